```python
import math
import jax
import jax.numpy as jnp
from jax import lax
import numpy as np

D_MODEL = 4096
BATCH = 1
SEQ = 8192
DEPTH = 4

CTX_LEN = 256
GRID_W = 64
F32 = jnp.float32

A_HEADS = 20
A_HEAD_DIM = 64
A_WIDTH = A_HEADS * A_HEAD_DIM
A_DECAY_RANK = 64
A_ICLR_RANK = 64
A_GATE_RANK = 192
A_IN = 3 * A_WIDTH + A_DECAY_RANK + A_ICLR_RANK + A_GATE_RANK
GN_EPS = 64e-5
B_WIDTH = 1280
B_IN = 3 * B_WIDTH
FILTER_EMB = 33
FILTER_BANDS = (FILTER_EMB - 1) // 2
FILTER_HIDDEN = 64
MIN_DECAY = math.log(1e-2) / 1.5
MAX_DECAY = math.log(1e-2) / 0.3
C_HEADS = 12
QK_NOPE = 128
QK_ROPE = 64
QK_DIM = QK_NOPE + QK_ROPE
V_HEAD = 128
C_WIDTH = C_HEADS * V_HEAD
Q_RANK = 1024
KV_RANK = 512
C_IN = Q_RANK + KV_RANK + QK_ROPE
ROPE_THETA = 10000.0
Q_BLOCK = 128
D_MIX = A_WIDTH + B_WIDTH + C_WIDTH
N_IN = A_IN + B_IN + C_IN
FFN_HIDDEN = 6144
ADA_RANK = 256
N_MOD = 9
NORM_EPS = 1e-6

kernel_name = 'hybrid_rwkv7_hyena_mla_dit_block'


def rms_norm(x, gain):
    xf = x.astype(F32)
    xf = xf * lax.rsqrt(jnp.mean(xf * xf, axis=-1, keepdims=True) + NORM_EPS)
    return (xf * gain.astype(F32)).astype(x.dtype)


def modulate(x, gain, shift, scale):
    return rms_norm(x, gain) * (1 + scale) + shift


def swiglu(h, w_gu, w_down):
    gate, up = jnp.split(h @ w_gu, 2, axis=-1)
    return (jax.nn.silu(gate) * up) @ w_down


def short_conv3(u, w, b=None):
    up = jnp.pad(u, ((0, 0), (1, 1), (0, 0)))
    y = up[:, :-2] * w[0] + up[:, 1:-1] * w[1] + up[:, 2:] * w[2]
    return y if b is None else y + b


def heads(t, n):
    return t.reshape(t.shape[0], t.shape[1], n, -1)


def rwkv_prep(u_proj, p):
    u = short_conv3(u_proj, p['a_conv'])
    r, k, v, xw, xa, xg = jnp.split(
        u, [A_WIDTH, 2 * A_WIDTH, 3 * A_WIDTH, 3 * A_WIDTH + A_DECAY_RANK,
            3 * A_WIDTH + A_DECAY_RANK + A_ICLR_RANK], axis=-1)
    kk = heads(k * p['a_k_k'], A_HEADS).astype(F32)
    kk = kk / jnp.maximum(jnp.sqrt(jnp.sum(kk * kk, axis=-1, keepdims=True)), 1e-12)
    kk = kk.reshape(k.shape)
    dirs = []
    for d in range(2):
        logw = -jax.nn.softplus(-(p['a_w0'][d] + jnp.tanh(xw) @ p['a_w_up'][d])) - 0.5
        decay = jnp.exp(-jnp.exp(logw.astype(F32)))
        a = jax.nn.sigmoid(p['a_a0'][d] + xa @ p['a_a_up'][d])
        k_d = k * (1 + (a - 1) * p['a_k_a'])
        dirs.append((decay, a, k_d))
    g = jax.nn.sigmoid(xg) @ p['a_g_up']
    return r, v, kk, g, dirs


def wkv7_scan(state0, r, decay, k, v, kk, a, reverse):
    B, L = r.shape[0], r.shape[1]

    def tm(t):
        return jnp.moveaxis(heads(t.astype(F32), A_HEADS), 1, 0)

    def step(S, inp):
        r_t, w_t, k_t, v_t, kk_t, a_t = inp
        sa = jnp.einsum('bhvk,bhk->bhv', S, -kk_t)
        S = (S * w_t[:, :, None, :] + sa[..., None] * (kk_t * a_t)[:, :, None, :]
             + v_t[..., None] * k_t[:, :, None, :])
        return S, jnp.einsum('bhvk,bhk->bhv', S, r_t)

    xs = (tm(r), tm(decay), tm(k), tm(v), tm(kk), tm(a))
    S, ys = lax.scan(step, state0, xs, reverse=reverse)
    return S, jnp.moveaxis(ys, 0, 1).reshape(B, L, A_WIDTH)


def rwkv_out(y, r, k_mean, v, g, p):
    B, L, _ = y.shape
    yh = heads(y, A_HEADS)
    mu = jnp.mean(yh, axis=-1, keepdims=True)
    var = jnp.mean(jnp.square(yh - mu), axis=-1, keepdims=True)
    yn = ((yh - mu) * lax.rsqrt(var + GN_EPS)).reshape(B, L, A_WIDTH)
    yn = yn * p['a_ln_w'] + p['a_ln_b']
    bonus = jnp.sum(heads((r * k_mean * p['a_r_k']).astype(F32), A_HEADS), axis=-1, keepdims=True)
    bonus = (bonus * heads(v.astype(F32), A_HEADS)).reshape(B, L, A_WIDTH)
    return ((yn + bonus) * g).astype(r.dtype)


def rwkv_mixer(ux, uc, p, need_ctx):
    r_c, v_c, kk_c, g_c, dirs_c = rwkv_prep(uc, p)
    r_x, v_x, kk_x, g_x, dirs_x = rwkv_prep(ux, p)
    zeros = jnp.zeros((ux.shape[0], A_HEADS, A_HEAD_DIM, A_HEAD_DIM), F32)
    y_x, y_c = 0.0, 0.0
    for d, reverse in enumerate((False, True)):
        dec, a, k = dirs_c[d]
        S_c, yc = wkv7_scan(zeros, r_c, dec, k, v_c, kk_c, a, reverse)
        dec, a, k = dirs_x[d]
        _, yx = wkv7_scan(S_c, r_x, dec, k, v_x, kk_x, a, reverse)
        y_x = y_x + yx
        y_c = y_c + yc
    o_x = rwkv_out(y_x, r_x, 0.5 * (dirs_x[0][2] + dirs_x[1][2]), v_x, g_x, p)
    if not need_ctx:
        return o_x, None
    o_c = rwkv_out(y_c, r_c, 0.5 * (dirs_c[0][2] + dirs_c[1][2]), v_c, g_c, p)
    return o_x, o_c


def hyena_filter(L, p):
    t = jnp.linspace(0.0, 1.0, L, dtype=F32)[:, None]
    w = (2.0 * math.pi / L) * jnp.arange(L, dtype=F32)[:, None]
    f = jnp.linspace(1e-4, FILTER_BANDS - 1, FILTER_BANDS, dtype=F32)[None, :]
    z = jnp.concatenate([t, jnp.cos(f * w), -jnp.sin(f * w)], axis=-1)
    freq = p['b_freq']
    h = jnp.sin(freq * (z @ p['b_fw1'] + p['b_fb1']))
    h = jnp.sin(freq * (h @ p['b_fw2'] + p['b_fb2']))
    h = jnp.sin(freq * (h @ p['b_fw3'] + p['b_fb3']))
    h = (h @ p['b_fw4']).astype(F32)
    deltas = jnp.abs(jnp.linspace(MIN_DECAY, MAX_DECAY, B_WIDTH, dtype=F32))
    h = h * jnp.exp(-t * jnp.tile(deltas, 2)[None, :])
    h_fwd, h_bwd = h[:, :B_WIDTH], h[:L - 1, B_WIDTH:]
    taps = jnp.concatenate([h_fwd, jnp.zeros((1, B_WIDTH), F32), h_bwd[::-1]], axis=0)
    return taps / jnp.sum(jnp.abs(taps), axis=0, keepdims=True)


def hyena(u_proj, p):
    u = short_conv3(u_proj, p['b_conv'], p['b_conv_b'])
    x0, x1, v = jnp.split(u, 3, axis=-1)
    L = u.shape[1]
    g = x1 * v
    taps = hyena_filter(L, p)
    y = jnp.fft.irfft(jnp.fft.rfft(g.astype(F32), n=2 * L, axis=1) * jnp.fft.rfft(taps, axis=0)[None],
                      n=2 * L, axis=1)[:, :L]
    return x0 * (y.astype(g.dtype) + g * p['b_bias'])


def axial_rope_tables(rows):
    row = jnp.broadcast_to(jnp.arange(rows, dtype=F32)[:, None], (rows, GRID_W)).reshape(-1)
    col = jnp.broadcast_to(jnp.arange(GRID_W, dtype=F32)[None, :], (rows, GRID_W)).reshape(-1)
    half = QK_ROPE // 2
    inv = ROPE_THETA ** (-jnp.arange(0, half, 2, dtype=F32) / half)
    ang_r = row[:, None] * inv[None]
    ang_c = col[:, None] * inv[None]
    ang = jnp.concatenate([ang_r, ang_r, ang_c, ang_c], axis=-1)
    return jnp.cos(ang), jnp.sin(ang)


def rotate(t, cos, sin):
    half = QK_ROPE // 2

    def rh(u):
        u1, u2 = jnp.split(u, 2, axis=-1)
        return jnp.concatenate([-u2, u1], axis=-1)

    rot = jnp.concatenate([rh(t[..., :half]), rh(t[..., half:])], axis=-1)
    return t * cos + rot * sin


def mla_qkv(u_proj, p, rope):
    cq, ckv, kpe = jnp.split(u_proj, [Q_RANK, Q_RANK + KV_RANK], axis=-1)
    q = heads(rms_norm(cq, p['c_q_norm']) @ p['c_q_up'], C_HEADS)
    kv = heads(rms_norm(ckv, p['c_kv_norm']) @ p['c_kv_up'], C_HEADS)
    q_nope = rms_norm(q[..., :QK_NOPE], p['c_qn_nope'])
    q_pe = rms_norm(q[..., QK_NOPE:], p['c_qn_pe'])
    k_nope = rms_norm(kv[..., :QK_NOPE], p['c_kn_nope'])
    v = kv[..., QK_NOPE:]
    k_pe = rms_norm(kpe, p['c_kn_pe'])
    if rope is not None:
        cos, sin = rope
        q_pe = rotate(q_pe, cos[None, :, None], sin[None, :, None])
        k_pe = rotate(k_pe, cos[None], sin[None])
    k_pe = jnp.broadcast_to(k_pe[:, :, None], k_nope.shape[:-1] + (QK_ROPE,))
    q = jnp.concatenate([q_nope, q_pe], axis=-1)
    k = jnp.concatenate([k_nope, k_pe], axis=-1)
    return q, k, v


def attend(q, k, v):
    s = jnp.einsum('bqhd,bkhd->bhqk', q, k).astype(F32) * (1.0 / math.sqrt(QK_DIM))
    pr = jax.nn.softmax(s, axis=-1).astype(v.dtype)
    return jnp.einsum('bhqk,bkhd->bqhd', pr, v)


def latent_attention(q_x, k_all, v_all):
    B, L, H, dk = q_x.shape
    nb = L // Q_BLOCK
    qb = jnp.moveaxis(q_x.reshape(B, nb, Q_BLOCK, H, dk), 1, 0)
    ob = lax.map(lambda qi: attend(qi, k_all, v_all), qb)
    return jnp.moveaxis(ob, 0, 1).reshape(B, L, C_WIDTH)


def mixer(px, pc, p, rope, need_ctx):
    ax, bx, cx = jnp.split(px, [A_IN, A_IN + B_IN], axis=-1)
    ac, bc, cc = jnp.split(pc, [A_IN, A_IN + B_IN], axis=-1)
    oa_x, oa_c = rwkv_mixer(ax, ac, p, need_ctx)
    ob_x = hyena(bx, p)
    q_c, k_c, v_c = mla_qkv(cc, p, None)
    q_x, k_x, v_x = mla_qkv(cx, p, rope)
    oc_x = latent_attention(q_x, jnp.concatenate([k_c, k_x], axis=1), jnp.concatenate([v_c, v_x], axis=1))
    o_x = jnp.concatenate([oa_x, ob_x.astype(oa_x.dtype), oc_x.astype(oa_x.dtype)], axis=-1)
    if not need_ctx:
        return o_x, None
    oc_c = attend(q_c, k_c, v_c).reshape(pc.shape[0], pc.shape[1], C_WIDTH)
    o_c = jnp.concatenate([oa_c, hyena(bc, p).astype(oa_c.dtype), oc_c.astype(oa_c.dtype)], axis=-1)
    return o_x, o_c


def layer(x, ctx, c, c_ctx, p, rope, need_ctx):
    mod_x = (jax.nn.silu(c) @ p['ada_down']) @ p['ada_up'] + p['ada_bias']
    mod_c = (jax.nn.silu(c_ctx) @ p['ada_down']) @ p['ada_up'] + p['ada_bias']
    mx = jnp.split(mod_x[:, None, :], N_MOD, axis=-1)
    mc = jnp.split(mod_c, N_MOD, axis=-1)
    x = x + 0.5 * mx[2] * swiglu(modulate(x, p['norm_ffn1'], mx[0], mx[1]), p['ffn1_gu'], p['ffn1_down'])
    ctx = ctx + 0.5 * mc[2] * swiglu(modulate(ctx, p['norm_ffn1'], mc[0], mc[1]), p['ffn1_gu'], p['ffn1_down'])
    px = modulate(x, p['norm_mix'], mx[3], mx[4]) @ p['w_in']
    pc = modulate(ctx, p['norm_mix'], mc[3], mc[4]) @ p['w_in']
    o_x, o_c = mixer(px, pc, p, rope, need_ctx)
    x = x + mx[5] * (o_x @ p['w_out'])
    x = x + 0.5 * mx[8] * swiglu(modulate(x, p['norm_ffn2'], mx[6], mx[7]), p['ffn2_gu'], p['ffn2_down'])
    if need_ctx:
        ctx = ctx + mc[5] * (o_c @ p['w_out'])
        ctx = ctx + 0.5 * mc[8] * swiglu(modulate(ctx, p['norm_ffn2'], mc[6], mc[7]), p['ffn2_gu'], p['ffn2_down'])
    return x, ctx


def setup_inputs(seed: int = 0) -> dict:
    key = jax.random.key(seed)
    keys = jax.random.split(key, 64)
    counter = [0]

    def nxt():
        k = keys[counter[0]]
        counter[0] += 1
        return k

    def nrm(shape, scale):
        return scale * jax.random.normal(nxt(), shape, F32)

    def gain(shape):
        return 1.0 + nrm(shape, 0.05)

    conv_base = jnp.array([0.2, 0.6, 0.2], F32)[None, :, None]
    Dp = DEPTH
    return {
        'x': nrm((BATCH, SEQ, D_MODEL), 1.0),
        'c': nrm((BATCH, D_MODEL), 1.0),
        'ctx': nrm((BATCH, CTX_LEN, D_MODEL), 1.0),
        'c_ctx': nrm((D_MODEL,), 1.0),
        'ada_down': nrm((Dp, D_MODEL, ADA_RANK), D_MODEL ** -0.5),
        'ada_up': nrm((Dp, ADA_RANK, N_MOD * D_MODEL), 0.5 * ADA_RANK ** -0.5),
        'ada_bias': nrm((Dp, N_MOD * D_MODEL), 0.02),
        'norm_ffn1': gain((Dp, D_MODEL)),
        'norm_mix': gain((Dp, D_MODEL)),
        'norm_ffn2': gain((Dp, D_MODEL)),
        'ffn1_gu': nrm((Dp, D_MODEL, 2 * FFN_HIDDEN), D_MODEL ** -0.5),
        'ffn1_down': nrm((Dp, FFN_HIDDEN, D_MODEL), FFN_HIDDEN ** -0.5),
        'ffn2_gu': nrm((Dp, D_MODEL, 2 * FFN_HIDDEN), D_MODEL ** -0.5),
        'ffn2_down': nrm((Dp, FFN_HIDDEN, D_MODEL), FFN_HIDDEN ** -0.5),
        'w_in': nrm((Dp, D_MODEL, N_IN), D_MODEL ** -0.5),
        'w_out': nrm((Dp, D_MIX, D_MODEL), D_MIX ** -0.5),
        'a_conv': conv_base + nrm((Dp, 3, A_IN), 0.1),
        'a_w0': jax.random.uniform(nxt(), (Dp, 2, A_WIDTH), F32, -6.0, -1.0),
        'a_w_up': nrm((Dp, 2, A_DECAY_RANK, A_WIDTH), 0.1),
        'a_a0': nrm((Dp, 2, A_WIDTH), 0.1),
        'a_a_up': nrm((Dp, 2, A_ICLR_RANK, A_WIDTH), 0.5 * A_ICLR_RANK ** -0.5),
        'a_g_up': nrm((Dp, A_GATE_RANK, A_WIDTH), A_GATE_RANK ** -0.5),
        'a_k_k': 0.85 + nrm((Dp, A_WIDTH), 0.05),
        'a_k_a': gain((Dp, A_WIDTH)),
        'a_r_k': nrm((Dp, A_WIDTH), 0.1),
        'a_ln_w': gain((Dp, A_WIDTH)),
        'a_ln_b': nrm((Dp, A_WIDTH), 0.02),
        'b_conv': conv_base + nrm((Dp, 3, B_IN), 0.1),
        'b_conv_b': nrm((Dp, B_IN), 0.02),
        'b_fw1': nrm((Dp, FILTER_EMB, FILTER_HIDDEN), FILTER_EMB ** -0.5),
        'b_fb1': nrm((Dp, FILTER_HIDDEN), 0.1),
        'b_fw2': nrm((Dp, FILTER_HIDDEN, FILTER_HIDDEN), FILTER_HIDDEN ** -0.5),
        'b_fb2': nrm((Dp, FILTER_HIDDEN), 0.1),
        'b_fw3': nrm((Dp, FILTER_HIDDEN, FILTER_HIDDEN), FILTER_HIDDEN ** -0.5),
        'b_fb3': nrm((Dp, FILTER_HIDDEN), 0.1),
        'b_fw4': nrm((Dp, FILTER_HIDDEN, 2 * B_WIDTH), FILTER_HIDDEN ** -0.5),
        'b_freq': 1.0 + nrm((Dp, FILTER_HIDDEN), 0.1),
        'b_bias': nrm((Dp, B_WIDTH), 1.0),
        'c_q_norm': gain((Dp, Q_RANK)),
        'c_q_up': nrm((Dp, Q_RANK, C_HEADS * QK_DIM), Q_RANK ** -0.5),
        'c_kv_norm': gain((Dp, KV_RANK)),
        'c_kv_up': nrm((Dp, KV_RANK, C_HEADS * (QK_NOPE + V_HEAD)), KV_RANK ** -0.5),
        'c_qn_nope': gain((Dp, QK_NOPE)),
        'c_qn_pe': gain((Dp, QK_ROPE)),
        'c_kn_nope': gain((Dp, QK_NOPE)),
        'c_kn_pe': gain((Dp, QK_ROPE)),
    }


def reference(x, c, ctx, c_ctx, ada_down, ada_up, ada_bias, norm_ffn1, norm_mix, norm_ffn2,
              ffn1_gu, ffn1_down, ffn2_gu, ffn2_down, w_in, w_out,
              a_conv, a_w0, a_w_up, a_a0, a_a_up, a_g_up, a_k_k, a_k_a, a_r_k, a_ln_w, a_ln_b,
              b_conv, b_conv_b, b_fw1, b_fb1, b_fw2, b_fb2, b_fw3, b_fb3, b_fw4, b_freq, b_bias,
              c_q_norm, c_q_up, c_kv_norm, c_kv_up, c_qn_nope, c_qn_pe, c_kn_nope, c_kn_pe):
    ROWS = x.shape[1] // GRID_W
    rope = axial_rope_tables(ROWS)
    for l in range(DEPTH):
        p = {
            'ada_down': ada_down[l], 'ada_up': ada_up[l], 'ada_bias': ada_bias[l],
            'norm_ffn1': norm_ffn1[l], 'norm_mix': norm_mix[l], 'norm_ffn2': norm_ffn2[l],
            'ffn1_gu': ffn1_gu[l], 'ffn1_down': ffn1_down[l],
            'ffn2_gu': ffn2_gu[l], 'ffn2_down': ffn2_down[l],
            'w_in': w_in[l], 'w_out': w_out[l],
            'a_conv': a_conv[l], 'a_w0': a_w0[l], 'a_w_up': a_w_up[l], 'a_a0': a_a0[l],
            'a_a_up': a_a_up[l], 'a_g_up': a_g_up[l], 'a_k_k': a_k_k[l], 'a_k_a': a_k_a[l],
            'a_r_k': a_r_k[l], 'a_ln_w': a_ln_w[l], 'a_ln_b': a_ln_b[l],
            'b_conv': b_conv[l], 'b_conv_b': b_conv_b[l], 'b_fw1': b_fw1[l], 'b_fb1': b_fb1[l],
            'b_fw2': b_fw2[l], 'b_fb2': b_fb2[l], 'b_fw3': b_fw3[l], 'b_fb3': b_fb3[l],
            'b_fw4': b_fw4[l], 'b_freq': b_freq[l], 'b_bias': b_bias[l],
            'c_q_norm': c_q_norm[l], 'c_q_up': c_q_up[l], 'c_kv_norm': c_kv_norm[l],
            'c_kv_up': c_kv_up[l], 'c_qn_nope': c_qn_nope[l], 'c_qn_pe': c_qn_pe[l],
            'c_kn_nope': c_kn_nope[l], 'c_kn_pe': c_kn_pe[l],
        }
        x, ctx = layer(x, ctx, c, c_ctx, p, rope, l < DEPTH - 1)
    return x
```

```python
import functools
import math

import jax
import jax.numpy as jnp
from jax import lax
from jax.experimental import pallas as pl
from jax.experimental.pallas import tpu as pltpu

F32 = jnp.float32
BF16 = jnp.bfloat16

D_MODEL = 4096
GRID_W = 64
A_HEADS = 20
A_HEAD_DIM = 64
A_WIDTH = A_HEADS * A_HEAD_DIM
A_DECAY_RANK = 64
A_ICLR_RANK = 64
A_GATE_RANK = 192
A_IN = 3 * A_WIDTH + A_DECAY_RANK + A_ICLR_RANK + A_GATE_RANK
GN_EPS = 64e-5
B_WIDTH = 1280
B_IN = 3 * B_WIDTH
FILTER_EMB = 33
FILTER_BANDS = (FILTER_EMB - 1) // 2
FILTER_HIDDEN = 64
MIN_DECAY = math.log(1e-2) / 1.5
MAX_DECAY = math.log(1e-2) / 0.3
C_HEADS = 12
QK_NOPE = 128
QK_ROPE = 64
QK_DIM = QK_NOPE + QK_ROPE
V_HEAD = 128
C_WIDTH = C_HEADS * V_HEAD
Q_RANK = 1024
KV_RANK = 512
C_IN = Q_RANK + KV_RANK + QK_ROPE
ROPE_THETA = 10000.0
FFN_HIDDEN = 6144
N_MOD = 9
NORM_EPS = 1e-6

LANES = 128
SUBLANES = 8
VMEM_LIMIT = 56 * 1024 * 1024

A_PAD = 4224
C_PAD = 1664
QH = 256
N_PAIRS = A_HEADS // 2
WKV_T = 128
FFT_N2 = 128


def _cparams(sem):
    return pltpu.CompilerParams(dimension_semantics=sem, vmem_limit_bytes=VMEM_LIMIT)


def _pick(n, cands):
    for c in cands:
        if n % c == 0:
            return c
    raise ValueError(f"no tile for {n}")


def _dot(a, b):
    return jnp.dot(a, b, preferred_element_type=F32)


def _split(a):
    hi = a.astype(BF16)
    lo = (a - hi.astype(F32)).astype(BF16)
    return hi, lo


def _dot3(a, b):
    ah, al = _split(a)
    bh, bl = _split(b)
    return _dot(ah, bh) + _dot(al, bh) + _dot(ah, bl)


def _dot2(a, b_bf16):
    ah, al = _split(a)
    return _dot(ah, b_bf16) + _dot(al, b_bf16)


def _sigmoid(x):
    return 1.0 / (1.0 + jnp.exp(-x))


def _silu(x):
    return x * _sigmoid(x)


def _softplus(x):
    return jnp.maximum(x, 0.0) + jnp.log(1.0 + jnp.exp(-jnp.abs(x)))


def _mm_plain_kernel(a_ref, b_ref, o_ref, acc_ref):
    k = pl.program_id(2)

    @pl.when(k == 0)
    def _():
        acc_ref[...] = jnp.zeros_like(acc_ref)

    acc_ref[...] += _dot(a_ref[...], b_ref[...])

    @pl.when(k == pl.num_programs(2) - 1)
    def _():
        o_ref[...] = acc_ref[...].astype(o_ref.dtype)


def _mm_bias_kernel(a_ref, b_ref, bias_ref, o_ref, acc_ref):
    k = pl.program_id(2)

    @pl.when(k == 0)
    def _():
        acc_ref[...] = jnp.zeros_like(acc_ref)

    acc_ref[...] += _dot(a_ref[...], b_ref[...])

    @pl.when(k == pl.num_programs(2) - 1)
    def _():
        o_ref[...] = (acc_ref[...] + bias_ref[...]).astype(o_ref.dtype)


def _mm_swiglu_kernel(a_ref, bg_ref, bu_ref, o_ref, accg_ref, accu_ref):
    k = pl.program_id(2)

    @pl.when(k == 0)
    def _():
        accg_ref[...] = jnp.zeros_like(accg_ref)
        accu_ref[...] = jnp.zeros_like(accu_ref)

    a = a_ref[...]
    accg_ref[...] += _dot(a, bg_ref[...])
    accu_ref[...] += _dot(a, bu_ref[...])

    @pl.when(k == pl.num_programs(2) - 1)
    def _():
        o_ref[...] = (_silu(accg_ref[...]) * accu_ref[...]).astype(o_ref.dtype)


def _mm_resid_kernel(a_ref, b_ref, res_ref, gate_ref, o_ref, acc_ref, *, coef, n_lat, tm):
    k = pl.program_id(2)

    @pl.when(k == 0)
    def _():
        acc_ref[...] = jnp.zeros_like(acc_ref)

    acc_ref[...] += _dot(a_ref[...], b_ref[...])

    @pl.when(k == pl.num_programs(2) - 1)
    def _():
        rows = pl.program_id(0) * tm + lax.broadcasted_iota(jnp.int32, acc_ref.shape, 0)
        gate = jnp.where(rows < n_lat, gate_ref[0:1, :], gate_ref[1:2, :])
        o_ref[...] = res_ref[...] + coef * gate * acc_ref[...]


def _mm(a, b, *, mode="plain", out_dtype=F32, tn=None, tk=None, bias=None, res=None, gate=None,
        coef=1.0, n_lat=0):
    M, K = a.shape
    N = b.shape[1] // 2 if mode == "swiglu" else b.shape[1]
    tm = _pick(M, (768, 1024, 640, 512, 256, 128, 16))
    tn = tn or _pick(N, (1024, 768, 512, 384, 256, 128))
    tk = tk or _pick(K, (512, 256, 128))
    grid = (M // tm, N // tn, K // tk)
    a_spec = pl.BlockSpec((tm, tk), lambda i, j, k: (i, k))
    b_spec = pl.BlockSpec((tk, tn), lambda i, j, k: (k, j))
    o_spec = pl.BlockSpec((tm, tn), lambda i, j, k: (i, j))
    acc = pltpu.VMEM((tm, tn), F32)
    if mode == "plain":
        kern, in_specs, args, scratch = _mm_plain_kernel, [a_spec, b_spec], (a, b), [acc]
    elif mode == "bias":
        kern = _mm_bias_kernel
        in_specs = [a_spec, b_spec, pl.BlockSpec((1, tn), lambda i, j, k: (0, j))]
        args, scratch = (a, b, bias), [acc]
    elif mode == "swiglu":
        nj = N // tn
        kern = _mm_swiglu_kernel
        in_specs = [a_spec, b_spec, pl.BlockSpec((tk, tn), lambda i, j, k: (k, j + nj))]
        args, scratch = (a, b, b), [acc, acc]
    else:
        kern = functools.partial(_mm_resid_kernel, coef=coef, n_lat=n_lat, tm=tm)
        in_specs = [a_spec, b_spec, o_spec, pl.BlockSpec((2, tn), lambda i, j, k: (0, j))]
        args, scratch = (a, b, res, gate), [acc]
    return pl.pallas_call(
        kern,
        grid=grid,
        in_specs=in_specs,
        out_specs=o_spec,
        out_shape=jax.ShapeDtypeStruct((M, N), out_dtype),
        scratch_shapes=scratch,
        compiler_params=_cparams(("parallel", "parallel", "arbitrary")),
        name=f"mm_{mode}",
    )(*args)


def _rowwise(fn, toks, consts, outs, tm, *, halo=False, name="rowwise"):
    M = toks[0].shape[0]
    n_t, n_c = len(toks), len(consts)

    def kern(*refs):
        i = pl.program_id(0)
        vals = [r[...] for r in refs[:n_t + (2 if halo else 0) + n_c]]
        res = fn(i, *vals)
        for r, v in zip(refs[n_t + (2 if halo else 0) + n_c:], res):
            r[...] = v.astype(r.dtype)

    in_specs = [pl.BlockSpec((tm, t.shape[1]), lambda i: (i, 0)) for t in toks]
    args = list(toks)
    if halo:
        width = toks[0].shape[1]
        per, last = tm // SUBLANES, M // SUBLANES - 1
        in_specs.append(pl.BlockSpec((SUBLANES, width), lambda i: (jnp.maximum(i * per - 1, 0), 0)))
        in_specs.append(pl.BlockSpec((SUBLANES, width), lambda i: (jnp.minimum((i + 1) * per, last), 0)))
        args += [toks[0], toks[0]]
    for c in consts:
        in_specs.append(pl.BlockSpec(c.shape, lambda i, nd=c.ndim: (0,) * nd))
        args.append(c)
    return pl.pallas_call(
        kern,
        grid=(M // tm,),
        in_specs=in_specs,
        out_specs=[pl.BlockSpec((tm, w), lambda i: (i, 0)) for w, _ in outs],
        out_shape=[jax.ShapeDtypeStruct((M, w), dt) for w, dt in outs],
        compiler_params=_cparams(("parallel",)),
        name=name,
    )(*args)


def _row_ids(i, tm, shape):
    return i * tm + lax.broadcasted_iota(jnp.int32, shape, 0)


def _conv3(i, x, prev8, next8, w, tm, n_lat, n_all):
    rows = _row_ids(i, tm, x.shape)
    local = lax.broadcasted_iota(jnp.int32, x.shape, 0)
    prev_row = jnp.broadcast_to(prev8[SUBLANES - 1:SUBLANES, :], x.shape)
    next_row = jnp.broadcast_to(next8[0:1, :], x.shape)
    xp = jnp.where(local == 0, prev_row, pltpu.roll(x, 1, 0))
    xn = jnp.where(local == tm - 1, next_row, pltpu.roll(x, tm - 1, 0))
    xp = jnp.where((rows == 0) | (rows == n_lat), 0.0, xp)
    xn = jnp.where((rows == n_lat - 1) | (rows == n_all - 1), 0.0, xn)
    return xp * w[0:1, :] + x * w[1:2, :] + xn * w[2:3, :]


def _rms(x, gain, width):
    ms = jnp.sum(x * x, axis=-1, keepdims=True) * (1.0 / width)
    return x * lax.rsqrt(ms + NORM_EPS) * gain


def _wkv_kernel(rf, vf, kkf, wf, kf, bf, rb, vb, kkb, wb, kb, bb, eh_ref, dm_ref,
                yf_ref, yb_ref, s_ref, *, T):
    @pl.when(pl.program_id(0) == 0)
    def _():
        s_ref[...] = jnp.zeros_like(s_ref)

    yf_ref[...] = jnp.zeros_like(yf_ref)
    yb_ref[...] = jnp.zeros_like(yb_ref)
    eh = eh_ref[...]
    dm = dm_ref[...]
    tile = (A_HEAD_DIM, LANES)
    dirs = ((rf, vf, kkf, wf, kf, bf, yf_ref), (rb, vb, kkb, wb, kb, bb, yb_ref))

    def step(t, carry):
        for d, (r_, v_, kk_, w_, k_, b_, y_) in enumerate(dirs):
            tt = t if d == 0 else T - 1 - t
            rrow, vrow, kkrow = r_[pl.ds(tt, 1), :], v_[pl.ds(tt, 1), :], kk_[pl.ds(tt, 1), :]
            wrow, krow, brow = w_[pl.ds(tt, 1), :], k_[pl.ds(tt, 1), :], b_[pl.ds(tt, 1), :]
            for p in range(N_PAIRS):
                sl = slice(p * LANES, (p + 1) * LANES)
                s = s_ref[d, p]
                sa = _dot((s * jnp.broadcast_to(kkrow[:, sl], tile)).astype(BF16), eh)
                vcol = _dot2(jnp.broadcast_to(vrow[:, sl], tile) * dm, eh)
                s = (s * jnp.broadcast_to(wrow[:, sl], tile)
                     - sa * jnp.broadcast_to(brow[:, sl], tile)
                     + vcol * jnp.broadcast_to(krow[:, sl], tile))
                s_ref[d, p] = s
                yb = _dot((s * jnp.broadcast_to(rrow[:, sl], tile)).astype(BF16), eh)
                yrow = jnp.sum(yb * dm, axis=0, keepdims=True)
                base = pl.multiple_of((tt // SUBLANES) * SUBLANES, SUBLANES)
                sub = lax.broadcasted_iota(jnp.int32, (SUBLANES, LANES), 0)
                old = y_[pl.ds(base, SUBLANES), sl]
                y_[pl.ds(base, SUBLANES), sl] = jnp.where(
                    sub == tt % SUBLANES, jnp.broadcast_to(yrow, old.shape), old)
        return carry

    lax.fori_loop(0, T, step, 0)


def _wkv(r, v, kk, w0, k0, b0, w1, k1, b1, eh, dm, n_lat):
    M = r.shape[0]
    T = WKV_T
    nx, nb = n_lat // T, M // T
    nc = nb - nx

    def fwd(i):
        return (jnp.where(i < nc, nx + i, i - nc), 0)

    def bwd(i):
        return (jnp.where(i < nc, nx + (nc - 1 - i), nx - 1 - (i - nc)), 0)

    fspec = pl.BlockSpec((T, A_WIDTH), fwd)
    bspec = pl.BlockSpec((T, A_WIDTH), bwd)
    return pl.pallas_call(
        functools.partial(_wkv_kernel, T=T),
        grid=(nb,),
        in_specs=[fspec] * 6 + [bspec] * 6 + [
            pl.BlockSpec(eh.shape, lambda i: (0, 0)), pl.BlockSpec(dm.shape, lambda i: (0, 0))],
        out_specs=[fspec, bspec],
        out_shape=[jax.ShapeDtypeStruct((M, A_WIDTH), F32)] * 2,
        scratch_shapes=[pltpu.VMEM((2, N_PAIRS, A_HEAD_DIM, LANES), F32)],
        compiler_params=_cparams(("arbitrary",)),
        name="wkv7_scan",
    )(r, v, kk, w0, k0, b0, r, v, kk, w1, k1, b1, eh, dm)


def _dft_rows_kernel(fc_ref, fs_ref, x_ref, ar_ref, ai_ref):
    x = x_ref[...]
    ar_ref[...] = _dot3(fc_ref[...], x)
    ai_ref[...] = -_dot3(fs_ref[...], x)


def _dft_rows(fc, fs, x2d):
    n1h, W = x2d.shape
    n1 = fc.shape[0]
    tn = _pick(W, (4096, 2048, 1024, 512, 256, 128))
    fspec = pl.BlockSpec(fc.shape, lambda j: (0, 0))
    return pl.pallas_call(
        _dft_rows_kernel,
        grid=(W // tn,),
        in_specs=[fspec, fspec, pl.BlockSpec((n1h, tn), lambda j: (0, j))],
        out_specs=[pl.BlockSpec((n1, tn), lambda j: (0, j))] * 2,
        out_shape=[jax.ShapeDtypeStruct((n1, W), F32)] * 2,
        compiler_params=_cparams(("parallel",)),
        name="fft_stage1",
    )(fc, fs, x2d)


def _tile_lanes(t, width):
    return t if width == LANES else jnp.concatenate([t] * (width // LANES), axis=-1)


def _fft_fwd_mid(ar, ai, twr, twi, fc, fs):
    br = ar * twr - ai * twi
    bi = ar * twi + ai * twr
    xr = _dot3(fc, br) + _dot3(fs, bi)
    xi = _dot3(fc, bi) - _dot3(fs, br)
    return xr, xi


def _filter_spec_kernel(afr, afi, apr, api, twr_ref, twi_ref, fc_ref, fs_ref, hr_ref, hi_ref):
    ct = afr.shape[-1]
    twr, twi = _tile_lanes(twr_ref[0], ct), _tile_lanes(twi_ref[0], ct)
    fc, fs = fc_ref[...], fs_ref[...]
    fr, fi = _fft_fwd_mid(afr[0], afi[0], twr, twi, fc, fs)
    pr, pi = _fft_fwd_mid(apr[0], api[0], twr, twi, fc, fs)
    hr_ref[0] = fr + pr
    hi_ref[0] = fi - pi


def _conv_spec_kernel(agr, agi, hr, hi, twr_ref, twi_ref, fc_ref, fs_ref, dr_ref, di_ref):
    ct = agr.shape[-1]
    twr, twi = _tile_lanes(twr_ref[0], ct), _tile_lanes(twi_ref[0], ct)
    fc, fs = fc_ref[...], fs_ref[...]
    xr, xi = _fft_fwd_mid(agr[0], agi[0], twr, twi, fc, fs)
    yr = xr * hr[0] - xi * hi[0]
    yi = xr * hi[0] + xi * hr[0]
    cr = _dot3(fc, yr) - _dot3(fs, yi)
    ci = _dot3(fc, yi) + _dot3(fs, yr)
    dr_ref[0] = cr * twr + ci * twi
    di_ref[0] = ci * twr - cr * twi


def _idft_rows_kernel(gc_ref, gs_ref, dr_ref, di_ref, y_ref, *, scale):
    y_ref[...] = scale * (_dot3(gc_ref[...], dr_ref[...]) - _dot3(gs_ref[...], di_ref[...]))


def _idft_rows(gc, gs, dr2d, di2d, scale):
    n1, W = dr2d.shape
    n1h = gc.shape[0]
    tn = _pick(W, (4096, 2048, 1024, 512, 256, 128))
    gspec = pl.BlockSpec(gc.shape, lambda j: (0, 0))
    dspec = pl.BlockSpec((n1, tn), lambda j: (0, j))
    return pl.pallas_call(
        functools.partial(_idft_rows_kernel, scale=scale),
        grid=(W // tn,),
        in_specs=[gspec, gspec, dspec, dspec],
        out_specs=pl.BlockSpec((n1h, tn), lambda j: (0, j)),
        out_shape=jax.ShapeDtypeStruct((n1h, W), F32),
        compiler_params=_cparams(("parallel",)),
        name="fft_stage1_inv",
    )(gc, gs, dr2d, di2d)


def _fft_tables(n_lat):
    n = 2 * n_lat
    n1 = n // FFT_N2

    def cs(rows, cols, period):
        idx = (jnp.arange(rows, dtype=jnp.int32)[:, None] * jnp.arange(cols, dtype=jnp.int32)[None, :]) % period
        ang = idx.astype(F32) * (2.0 * math.pi / period)
        return jnp.cos(ang), jnp.sin(ang)

    f1c, f1s = cs(n1, n1, n1)
    f2c, f2s = cs(FFT_N2, FFT_N2, FFT_N2)
    twc, tws = cs(n1, FFT_N2, n)
    shape = (n1, FFT_N2, LANES)
    twc = jnp.broadcast_to(twc[:, :, None], shape)
    tws = jnp.broadcast_to(tws[:, :, None], shape)
    return dict(n=n, n1=n1, f1c=f1c, f1s=f1s, f2c=f2c, f2s=f2s, twr=twc, twi=-tws)


def _long_conv(g_seqs, h_seqs, tabs):
    n_lat, C = g_seqs[0].shape
    n1, n = tabs["n1"], tabs["n"]
    ns = len(g_seqs)
    f1c_h, f1s_h = tabs["f1c"][:, :n1 // 2], tabs["f1s"][:, :n1 // 2]
    hf = jnp.concatenate([h[0] for h in h_seqs], axis=1)
    hp = jnp.concatenate([h[1] for h in h_seqs], axis=1)
    q2d = jnp.concatenate([hf, hp], axis=1).reshape(n1 // 2, FFT_N2 * 2 * ns * C)
    aqr, aqi = _dft_rows(f1c_h, f1s_h, q2d)
    aqr = aqr.reshape(n1, FFT_N2, 2 * ns * C)
    aqi = aqi.reshape(n1, FFT_N2, 2 * ns * C)
    ct = 256
    nj = ns * C // ct
    blk_f = pl.BlockSpec((1, FFT_N2, ct), lambda k, j: (k, 0, j))
    blk_p = pl.BlockSpec((1, FFT_N2, ct), lambda k, j: (k, 0, j + nj))
    tw = pl.BlockSpec((1, FFT_N2, LANES), lambda k, j: (k, 0, 0))
    fm = pl.BlockSpec((FFT_N2, FFT_N2), lambda k, j: (0, 0))
    hr, hi = pl.pallas_call(
        _filter_spec_kernel,
        grid=(n1, nj),
        in_specs=[blk_f, blk_f, blk_p, blk_p, tw, tw, fm, fm],
        out_specs=[blk_f, blk_f],
        out_shape=[jax.ShapeDtypeStruct((n1, FFT_N2, ns * C), F32)] * 2,
        compiler_params=_cparams(("parallel", "parallel")),
        name="fft_filter_spectrum",
    )(aqr, aqi, aqr, aqi, tabs["twr"], tabs["twi"], tabs["f2c"], tabs["f2s"])
    g2d = jnp.concatenate(g_seqs, axis=1).reshape(n1 // 2, FFT_N2 * ns * C)
    agr, agi = _dft_rows(f1c_h, f1s_h, g2d)
    agr = agr.reshape(n1, FFT_N2, ns * C)
    agi = agi.reshape(n1, FFT_N2, ns * C)
    dr, di = pl.pallas_call(
        _conv_spec_kernel,
        grid=(n1, nj),
        in_specs=[blk_f] * 4 + [tw, tw, fm, fm],
        out_specs=[blk_f, blk_f],
        out_shape=[jax.ShapeDtypeStruct((n1, FFT_N2, ns * C), F32)] * 2,
        compiler_params=_cparams(("parallel", "parallel")),
        name="fft_conv_spectrum",
    )(agr, agi, hr, hi, tabs["twr"], tabs["twi"], tabs["f2c"], tabs["f2s"])
    y2d = _idft_rows(tabs["f1c"][:n1 // 2, :], tabs["f1s"][:n1 // 2, :],
                     dr.reshape(n1, FFT_N2 * ns * C), di.reshape(n1, FFT_N2 * ns * C), 1.0 / n)
    y = y2d.reshape(n_lat, ns * C)
    return [y[:, i * C:(i + 1) * C] for i in range(ns)]


def _filter_feats(length):
    t = jnp.linspace(0.0, 1.0, length, dtype=F32)[:, None]
    w = (2.0 * math.pi / length) * jnp.arange(length, dtype=F32)[:, None]
    f = jnp.linspace(1e-4, FILTER_BANDS - 1, FILTER_BANDS, dtype=F32)[None, :]
    z = jnp.concatenate([t, jnp.cos(f * w), -jnp.sin(f * w)], axis=-1)
    return jnp.pad(z, ((0, 0), (0, LANES - FILTER_EMB)))


def _filter_kernel(z_ref, w1, b1, w2, b2, w3, b3, w4, fr, dl, h_ref, s_ref, *, tm, length):
    i = pl.program_id(0)
    z = z_ref[...]
    freq = fr[...]
    h = jnp.sin(freq * (_dot3(z, w1[...]) + b1[...]))
    h = jnp.sin(freq * (_dot3(h, w2[...]) + b2[...]))
    h = jnp.sin(freq * (_dot3(h, w3[...]) + b3[...]))
    h = _dot3(h, w4[...])
    h = h * jnp.exp(-z[:, 0:1] * dl[...])
    h_ref[...] = h
    rows = _row_ids(i, tm, h.shape)
    cols = lax.broadcasted_iota(jnp.int32, h.shape, 1)
    keep = (cols < B_WIDTH) | (rows < length - 1)
    part = jnp.sum(jnp.where(keep, jnp.abs(h), 0.0), axis=0, keepdims=True)

    @pl.when(i == 0)
    def _():
        s_ref[...] = jnp.zeros_like(s_ref)

    s_ref[...] += jnp.broadcast_to(part, s_ref.shape)


def _filter_taps(length, fp):
    z = _filter_feats(length)
    tm = _pick(length, (256, 128))
    consts = [fp["w1"], fp["b1"], fp["w2"], fp["b2"], fp["w3"], fp["b3"], fp["w4"], fp["freq"], fp["deltas"]]
    in_specs = [pl.BlockSpec((tm, LANES), lambda i: (i, 0))]
    in_specs += [pl.BlockSpec(c.shape, lambda i: (0, 0)) for c in consts]
    h, s = pl.pallas_call(
        functools.partial(_filter_kernel, tm=tm, length=length),
        grid=(length // tm,),
        in_specs=in_specs,
        out_specs=[pl.BlockSpec((tm, 2 * B_WIDTH), lambda i: (i, 0)),
                   pl.BlockSpec((SUBLANES, 2 * B_WIDTH), lambda i: (0, 0))],
        out_shape=[jax.ShapeDtypeStruct((length, 2 * B_WIDTH), F32),
                   jax.ShapeDtypeStruct((SUBLANES, 2 * B_WIDTH), F32)],
        compiler_params=_cparams(("arbitrary",)),
        name="hyena_filter",
    )(z, *consts)
    return h, s


def _attn_kernel(q_ref, k_ref, v_ref, o_ref, m_ref, l_ref, acc_ref):
    j = pl.program_id(2)

    @pl.when(j == 0)
    def _():
        m_ref[...] = jnp.full_like(m_ref, -jnp.inf)
        l_ref[...] = jnp.zeros_like(l_ref)
        acc_ref[...] = jnp.zeros_like(acc_ref)

    s = lax.dot_general(q_ref[...], k_ref[...], (((1,), (1,)), ((), ())), preferred_element_type=F32)
    m_old = m_ref[...]
    m_new = jnp.maximum(m_old, jnp.max(s, axis=-1, keepdims=True))
    alpha = jnp.exp(m_old - m_new)
    p = jnp.exp(s - m_new)
    l_ref[...] = alpha * l_ref[...] + jnp.sum(p, axis=-1, keepdims=True)
    acc_ref[...] = alpha * acc_ref[...] + _dot(p.astype(BF16), v_ref[...])
    m_ref[...] = m_new

    @pl.when(j == pl.num_programs(2) - 1)
    def _():
        o_ref[...] = (acc_ref[...] / l_ref[...]).astype(o_ref.dtype)


def _attention(q, k, v, q_row0, n_q, k_row0, n_k):
    tq = _pick(n_q, (1024, 512, 256))
    tk = _pick(n_k, (768, 1024, 640, 512, 256))
    assert q_row0 % tq == 0 and k_row0 % tk == 0
    qb, kb = q_row0 // tq, k_row0 // tk
    return pl.pallas_call(
        _attn_kernel,
        grid=(C_HEADS, n_q // tq, n_k // tk),
        in_specs=[pl.BlockSpec((tq, QH), lambda h, i, j: (qb + i, h)),
                  pl.BlockSpec((tk, QH), lambda h, i, j: (kb + j, h)),
                  pl.BlockSpec((tk, V_HEAD), lambda h, i, j: (kb + j, h))],
        out_specs=pl.BlockSpec((tq, V_HEAD), lambda h, i, j: (i, h)),
        out_shape=jax.ShapeDtypeStruct((n_q, C_WIDTH), BF16),
        scratch_shapes=[pltpu.VMEM((tq, 1), F32), pltpu.VMEM((tq, 1), F32), pltpu.VMEM((tq, V_HEAD), F32)],
        compiler_params=_cparams(("parallel", "parallel", "arbitrary")),
        name="mla_attention",
    )(q, k, v)


def _pair_sum(x, eh):
    return jnp.concatenate(
        [_dot2(x[:, p * LANES:(p + 1) * LANES], eh) for p in range(N_PAIRS)], axis=-1)


def _rot_half(t):
    lane = lax.broadcasted_iota(jnp.int32, t.shape, 1)
    up = pltpu.roll(t, LANES - 16, 1)
    down = pltpu.roll(t, 16, 1)
    return jnp.where(lane % 32 < 16, -up, down)


def _layer(xa, mod, p, const, n_lat, need_ctx):
    M = xa.shape[0]
    eh, dm = const["eh"], const["dm"]

    def modrow(k):
        return mod[:, k * D_MODEL:(k + 1) * D_MODEL]

    def norm_mod(x, gain, shift, scale):
        tm = 256

        def fn(i, xb, g, sh, sc):
            rows = _row_ids(i, tm, xb.shape)
            lat = rows < n_lat
            y = _rms(xb, g, D_MODEL)
            return (y * (1.0 + jnp.where(lat, sc[0:1], sc[1:2])) + jnp.where(lat, sh[0:1], sh[1:2]),)

        return _rowwise(fn, [x], [gain, shift, scale], [(D_MODEL, BF16)], tm, name="norm_modulate")[0]

    def ffn(x, gain, w_gu, w_down, k0):
        h = norm_mod(x, gain, modrow(k0), modrow(k0 + 1))
        act = _mm(h, w_gu, mode="swiglu", out_dtype=BF16, tn=512)
        return _mm(act, w_down, mode="resid", res=x, gate=modrow(k0 + 2), coef=0.5, n_lat=n_lat)

    xa = ffn(xa, p["norm_ffn1"], p["ffn1_gu"], p["ffn1_down"], 0)

    h = norm_mod(xa, p["norm_mix"], modrow(3), modrow(4))
    pa = _mm(h, p["w_in_a"], tn=1408)
    pb = _mm(h, p["w_in_b"], tn=1280)
    pc = _mm(h, p["w_in_c"], tn=C_PAD)

    tma = 128

    def rwkv_prep(i, x, prev8, next8, cw, k_k, k_a, w0, wup0, wup1, a0, aup0, aup1, gup, eh_):
        u = _conv3(i, x, prev8, next8, cw, tma, n_lat, M)
        r, k, v = u[:, :A_WIDTH], u[:, A_WIDTH:2 * A_WIDTH], u[:, 2 * A_WIDTH:3 * A_WIDTH]
        xwa = u[:, 3 * A_WIDTH:3 * A_WIDTH + LANES]
        xg = u[:, 3 * A_WIDTH + LANES:3 * A_WIDTH + 3 * LANES]
        kk = k * k_k
        kk = kk / jnp.maximum(jnp.sqrt(_pair_sum(kk * kk, eh_)), 1e-12)
        g = _dot3(_sigmoid(xg), gup)
        th = jnp.tanh(xwa)
        outs = [r, v, kk, g]
        for wup, aup, d in ((wup0, aup0, 0), (wup1, aup1, 1)):
            logw = -_softplus(-(w0[d:d + 1] + _dot3(th, wup))) - 0.5
            decay = jnp.exp(-jnp.exp(logw))
            a = _sigmoid(a0[d:d + 1] + _dot3(xwa, aup))
            outs += [decay, k * (1.0 + (a - 1.0) * k_a), kk * a]
        return outs

    r, v, kk, g, w0, k0, b0, w1, k1, b1 = _rowwise(
        rwkv_prep, [pa],
        [p["a_conv"], p["a_k_k"], p["a_k_a"], p["a_w0"], p["a_w_up0"], p["a_w_up1"], p["a_a0"],
         p["a_a_up0"], p["a_a_up1"], p["a_g_up"], eh],
        [(A_WIDTH, F32)] * 10, tma, halo=True, name="rwkv_prep")
    yf, yb = _wkv(r, v, kk, w0, k0, b0, w1, k1, b1, eh, dm, n_lat)

    def rwkv_out(i, yf_, yb_, r_, v_, g_, k0_, k1_, r_k, ln_w, ln_b, eh_):
        y = yf_ + yb_
        inv = 1.0 / A_HEAD_DIM
        mu = _pair_sum(y, eh_) * inv
        yc = y - mu
        var = _pair_sum(yc * yc, eh_) * inv
        yn = yc * lax.rsqrt(var + GN_EPS) * ln_w + ln_b
        bonus = _pair_sum(r_ * (0.5 * (k0_ + k1_)) * r_k, eh_) * v_
        return ((yn + bonus) * g_,)

    o_a = _rowwise(rwkv_out, [yf, yb, r, v, g, k0, k1], [p["a_r_k"], p["a_ln_w"], p["a_ln_b"], eh],
                   [(A_WIDTH, BF16)], 256, name="rwkv_out")[0]

    tmb = 256

    def hyena_prep(i, x, prev8, next8, cw, cb):
        u = _conv3(i, x, prev8, next8, cw, tmb, n_lat, M) + cb
        return u[:, :B_WIDTH], u[:, B_WIDTH:2 * B_WIDTH] * u[:, 2 * B_WIDTH:]

    x0, gsig = _rowwise(hyena_prep, [pb], [p["b_conv"], p["b_conv_b"]],
                        [(B_WIDTH, F32)] * 2, tmb, halo=True, name="hyena_prep")
    n_ctx = M - n_lat
    hx, sx = _filter_taps(n_lat, p["filt"])
    hc, sc = _filter_taps(n_ctx, p["filt"])

    def shifted(hb):
        return jnp.pad(hb[:-1], ((1, 0), (0, 0)))

    def padrows(t):
        return jnp.pad(t, ((0, n_lat - t.shape[0]), (0, 0)))

    ys = _long_conv(
        [gsig[:n_lat], padrows(gsig[n_lat:])],
        [(hx[:, :B_WIDTH], shifted(hx[:, B_WIDTH:])),
         (padrows(hc[:, :B_WIDTH]), padrows(shifted(hc[:, B_WIDTH:])))],
        const["fft"])
    yconv = jnp.concatenate([ys[0], ys[1][:n_ctx]], axis=0)

    def hyena_out(i, x0_, g_, y_, bias, sx_, sc_):
        rows = _row_ids(i, tmb, x0_.shape)
        nx_ = sx_[0:1, :B_WIDTH] + sx_[0:1, B_WIDTH:]
        nc_ = sc_[0:1, :B_WIDTH] + sc_[0:1, B_WIDTH:]
        norm = jnp.where(rows < n_lat, nx_, nc_)
        return (x0_ * (y_ / norm + g_ * bias),)

    o_b = _rowwise(hyena_out, [x0, gsig, yconv], [p["b_bias"], sx, sc], [(B_WIDTH, BF16)], tmb,
                   name="hyena_out")[0]

    tmc = 256

    def mla_prep(i, x, cos, sin, gq, gkv, gpe):
        cq, ckv = x[:, :Q_RANK], x[:, Q_RANK:Q_RANK + KV_RANK]
        kpe = x[:, Q_RANK + KV_RANK:]
        kpe = _rms(kpe, gpe, QK_ROPE)
        kpe = kpe * cos + _rot_half(kpe) * sin
        return _rms(cq, gq, Q_RANK), _rms(ckv, gkv, KV_RANK), kpe

    cqn, ckvn, kpe = _rowwise(mla_prep, [pc, const["cos"], const["sin"]],
                              [p["c_q_norm"], p["c_kv_norm"], p["c_kn_pe"]],
                              [(Q_RANK, BF16), (KV_RANK, BF16), (LANES, F32)], tmc, name="mla_prep")
    q_raw = _mm(cqn, p["c_q_up"], tn=768)
    kv_raw = _mm(ckvn, p["c_kv_up"], tn=768)
    scale = 1.0 / math.sqrt(QK_DIM)

    def mla_heads(i, qr, kvr, kpe_, cos, sin, gn_q, gpe_q, gn_k):
        qs, ks, vs = [], [], []
        for hh in range(C_HEADS):
            qn = _rms(qr[:, hh * QH:hh * QH + QK_NOPE], gn_q, QK_NOPE)
            qp = _rms(qr[:, hh * QH + QK_NOPE:(hh + 1) * QH], gpe_q, QK_ROPE)
            qp = qp * cos + _rot_half(qp) * sin
            qs += [qn * scale, qp * scale]
            ks += [_rms(kvr[:, hh * QH:hh * QH + QK_NOPE], gn_k, QK_NOPE), kpe_]
            vs.append(kvr[:, hh * QH + QK_NOPE:(hh + 1) * QH])
        return jnp.concatenate(qs, axis=-1), jnp.concatenate(ks, axis=-1), jnp.concatenate(vs, axis=-1)

    q, k, vv = _rowwise(mla_heads, [q_raw, kv_raw, kpe, const["cos"], const["sin"]],
                        [p["c_qn_nope"], p["c_qn_pe"], p["c_kn_nope"]],
                        [(C_HEADS * QH, BF16), (C_HEADS * QH, BF16), (C_WIDTH, BF16)], tmc, name="mla_heads")
    oc_x = _attention(q, k, vv, 0, n_lat, 0, M)
    if need_ctx:
        oc_c = _attention(q, k, vv, n_lat, n_ctx, n_lat, n_ctx)
    else:
        oc_c = jnp.zeros((n_ctx, C_WIDTH), BF16)
    o_c = jnp.concatenate([oc_x, oc_c], axis=0)

    o = jnp.concatenate([o_a, o_b, o_c], axis=-1)
    xa = _mm(o, p["w_out"], mode="resid", res=xa, gate=modrow(5), coef=1.0, n_lat=n_lat)
    return ffn(xa, p["norm_ffn2"], p["ffn2_gu"], p["ffn2_down"], 6)


def _pad_cols(w, width):
    return jnp.pad(w, [(0, 0)] * (w.ndim - 1) + [(0, width - w.shape[-1])])


def _pad_rows(w, height):
    return jnp.pad(w, [(0, 0)] * (w.ndim - 2) + [(0, height - w.shape[-2]), (0, 0)])


def _row(v):
    return v.reshape(1, -1)


def kernel(x, c, ctx, c_ctx, ada_down, ada_up, ada_bias, norm_ffn1, norm_mix, norm_ffn2, ffn1_gu, ffn1_down, ffn2_gu, ffn2_down, w_in, w_out, a_conv, a_w0, a_w_up, a_a0, a_a_up, a_g_up, a_k_k, a_k_a, a_r_k, a_ln_w, a_ln_b, b_conv, b_conv_b, b_fw1, b_fb1, b_fw2, b_fb2, b_fw3, b_fb3, b_fw4, b_freq, b_bias, c_q_norm, c_q_up, c_kv_norm, c_kv_up, c_qn_nope, c_qn_pe, c_kn_nope, c_kn_pe):
    n_lat, n_ctx = x.shape[1], ctx.shape[1]
    depth = w_in.shape[0]
    xa = jnp.concatenate([x[0], ctx[0]], axis=0)

    lane = jnp.arange(LANES)
    eh = (lane[:, None] // A_HEAD_DIM == lane[None, :] // A_HEAD_DIM).astype(BF16)
    dm = (lane[None, :] % A_HEAD_DIM == jnp.arange(A_HEAD_DIM)[:, None]).astype(F32)
    rows_g = n_lat // GRID_W
    row = jnp.broadcast_to(jnp.arange(rows_g, dtype=F32)[:, None], (rows_g, GRID_W)).reshape(-1)
    col = jnp.broadcast_to(jnp.arange(GRID_W, dtype=F32)[None, :], (rows_g, GRID_W)).reshape(-1)
    half = QK_ROPE // 2
    inv = ROPE_THETA ** (-jnp.arange(0, half, 2, dtype=F32) / half)
    ang = jnp.concatenate([row[:, None] * inv, row[:, None] * inv, col[:, None] * inv, col[:, None] * inv], axis=-1)
    cos = jnp.concatenate([jnp.cos(ang), jnp.ones((n_ctx, QK_ROPE), F32)], axis=0)
    sin = jnp.concatenate([jnp.sin(ang), jnp.zeros((n_ctx, QK_ROPE), F32)], axis=0)
    const = dict(eh=eh, dm=dm, cos=_pad_cols(cos, LANES), sin=_pad_cols(sin, LANES), fft=_fft_tables(n_lat))
    deltas = jnp.abs(jnp.linspace(MIN_DECAY, MAX_DECAY, B_WIDTH, dtype=F32))

    cc = jnp.pad(jnp.concatenate([c, c_ctx[None, :]], axis=0), ((0, 14), (0, 0)))
    cs = _rowwise(lambda i, t: (_silu(t),), [cc], [], [(D_MODEL, BF16)], 16, name="silu")[0]

    q_up = c_q_up.reshape(depth, Q_RANK, C_HEADS, QK_DIM)
    q_up = jnp.pad(q_up, ((0, 0), (0, 0), (0, 0), (0, QH - QK_DIM))).reshape(depth, Q_RANK, C_HEADS * QH)

    for l in range(depth):
        mid = _mm(cs, ada_down[l].astype(BF16), out_dtype=BF16, tn=256)
        mod = _mm(mid, ada_up[l].astype(BF16), mode="bias", bias=_row(ada_bias[l]), tn=1024, tk=256)[:2]
        w_in_l = w_in[l]
        filt = dict(
            w1=_pad_cols(_pad_rows(b_fw1[l], LANES), LANES), b1=_pad_cols(_row(b_fb1[l]), LANES),
            w2=_pad_cols(_pad_rows(b_fw2[l], LANES), LANES), b2=_pad_cols(_row(b_fb2[l]), LANES),
            w3=_pad_cols(_pad_rows(b_fw3[l], LANES), LANES), b3=_pad_cols(_row(b_fb3[l]), LANES),
            w4=_pad_rows(b_fw4[l], LANES), freq=_pad_cols(_row(b_freq[l]), LANES),
            deltas=_row(jnp.tile(deltas, 2)))
        zpad = jnp.zeros((A_DECAY_RANK, A_WIDTH), F32)
        p = dict(
            norm_ffn1=_row(norm_ffn1[l]), norm_mix=_row(norm_mix[l]), norm_ffn2=_row(norm_ffn2[l]),
            ffn1_gu=ffn1_gu[l].astype(BF16), ffn1_down=ffn1_down[l].astype(BF16),
            ffn2_gu=ffn2_gu[l].astype(BF16), ffn2_down=ffn2_down[l].astype(BF16),
            w_in_a=_pad_cols(w_in_l[:, :A_IN], A_PAD).astype(BF16),
            w_in_b=w_in_l[:, A_IN:A_IN + B_IN].astype(BF16),
            w_in_c=_pad_cols(w_in_l[:, A_IN + B_IN:], C_PAD).astype(BF16),
            w_out=w_out[l].astype(BF16),
            a_conv=_pad_cols(a_conv[l], A_PAD), a_k_k=_row(a_k_k[l]), a_k_a=_row(a_k_a[l]),
            a_w0=a_w0[l], a_a0=a_a0[l],
            a_w_up0=jnp.concatenate([a_w_up[l, 0], zpad], axis=0),
            a_w_up1=jnp.concatenate([a_w_up[l, 1], zpad], axis=0),
            a_a_up0=jnp.concatenate([zpad, a_a_up[l, 0]], axis=0),
            a_a_up1=jnp.concatenate([zpad, a_a_up[l, 1]], axis=0),
            a_g_up=_pad_rows(a_g_up[l], 2 * LANES),
            a_r_k=_row(a_r_k[l]), a_ln_w=_row(a_ln_w[l]), a_ln_b=_row(a_ln_b[l]),
            b_conv=b_conv[l], b_conv_b=_row(b_conv_b[l]), b_bias=_row(b_bias[l]), filt=filt,
            c_q_norm=_row(c_q_norm[l]), c_kv_norm=_row(c_kv_norm[l]),
            c_kn_pe=_pad_cols(_row(c_kn_pe[l]), LANES),
            c_q_up=q_up[l].astype(BF16), c_kv_up=c_kv_up[l].astype(BF16),
            c_qn_nope=_row(c_qn_nope[l]), c_qn_pe=_pad_cols(_row(c_qn_pe[l]), LANES),
            c_kn_nope=_row(c_kn_nope[l]),
        )
        xa = _layer(xa, mod, p, const, n_lat, l < depth - 1)
    return xa[:n_lat][None]
```

```python
import functools
import math

import jax
import jax.numpy as jnp
from jax import lax
from jax.experimental import pallas as pl
from jax.experimental.pallas import tpu as pltpu

F32 = jnp.float32
BF16 = jnp.bfloat16

D_MODEL = 4096
GRID_W = 64
A_HEADS = 20
A_HEAD_DIM = 64
A_WIDTH = A_HEADS * A_HEAD_DIM
A_DECAY_RANK = 64
A_ICLR_RANK = 64
A_GATE_RANK = 192
A_IN = 3 * A_WIDTH + A_DECAY_RANK + A_ICLR_RANK + A_GATE_RANK
GN_EPS = 64e-5
B_WIDTH = 1280
B_IN = 3 * B_WIDTH
FILTER_EMB = 33
FILTER_BANDS = (FILTER_EMB - 1) // 2
FILTER_HIDDEN = 64
MIN_DECAY = math.log(1e-2) / 1.5
MAX_DECAY = math.log(1e-2) / 0.3
C_HEADS = 12
QK_NOPE = 128
QK_ROPE = 64
QK_DIM = QK_NOPE + QK_ROPE
V_HEAD = 128
C_WIDTH = C_HEADS * V_HEAD
Q_RANK = 1024
KV_RANK = 512
C_IN = Q_RANK + KV_RANK + QK_ROPE
ROPE_THETA = 10000.0
FFN_HIDDEN = 6144
N_MOD = 9
NORM_EPS = 1e-6

LANES = 128
SUBLANES = 8
VMEM_LIMIT = 56 * 1024 * 1024

A_PAD = 4224
C_PAD = 1664
QH = 256
N_PAIRS = A_HEADS // 2
WKV_W = 256
N_QUADS = A_WIDTH // WKV_W
WKV_YW = 384
WKV_T = 128
WKV_UNROLL = 4
FFT_N2 = 128
ATTN_SUB = 256


def _cparams(sem):
    return pltpu.CompilerParams(dimension_semantics=sem, vmem_limit_bytes=VMEM_LIMIT)


def _pick(n, cands):
    for c in cands:
        if n % c == 0:
            return c
    raise ValueError(f"no tile for {n}")


def _dot(a, b):
    return jnp.dot(a, b, preferred_element_type=F32)


def _split(a):
    hi = a.astype(BF16)
    lo = (a - hi.astype(F32)).astype(BF16)
    return hi, lo


def _dot3(a, b):
    ah, al = _split(a)
    bh, bl = _split(b)
    return _dot(ah, bh) + _dot(al, bh) + _dot(ah, bl)


def _dot2(a, b_bf16):
    ah, al = _split(a)
    return _dot(ah, b_bf16) + _dot(al, b_bf16)


def _sigmoid(x):
    return 1.0 / (1.0 + jnp.exp(-x))


def _silu(x):
    return x * _sigmoid(x)


def _softplus(x):
    return jnp.maximum(x, 0.0) + jnp.log(1.0 + jnp.exp(-jnp.abs(x)))


def _mm_plain_kernel(a_ref, b_ref, o_ref, acc_ref):
    k = pl.program_id(2)

    @pl.when(k == 0)
    def _():
        acc_ref[...] = jnp.zeros_like(acc_ref)

    acc_ref[...] += _dot(a_ref[...], b_ref[...])

    @pl.when(k == pl.num_programs(2) - 1)
    def _():
        o_ref[...] = acc_ref[...].astype(o_ref.dtype)


def _mm_bias_kernel(a_ref, b_ref, bias_ref, o_ref, acc_ref):
    k = pl.program_id(2)

    @pl.when(k == 0)
    def _():
        acc_ref[...] = jnp.zeros_like(acc_ref)

    acc_ref[...] += _dot(a_ref[...], b_ref[...])

    @pl.when(k == pl.num_programs(2) - 1)
    def _():
        o_ref[...] = (acc_ref[...] + bias_ref[...]).astype(o_ref.dtype)


def _mm_swiglu_kernel(a_ref, bg_ref, bu_ref, o_ref, accg_ref, accu_ref):
    k = pl.program_id(2)

    @pl.when(k == 0)
    def _():
        accg_ref[...] = jnp.zeros_like(accg_ref)
        accu_ref[...] = jnp.zeros_like(accu_ref)

    a = a_ref[...]
    accg_ref[...] += _dot(a, bg_ref[...])
    accu_ref[...] += _dot(a, bu_ref[...])

    @pl.when(k == pl.num_programs(2) - 1)
    def _():
        o_ref[...] = (_silu(accg_ref[...]) * accu_ref[...]).astype(o_ref.dtype)


def _mm_resid_kernel(a_ref, b_ref, res_ref, gate_ref, o_ref, acc_ref, *, coef, n_lat, tm):
    k = pl.program_id(2)

    @pl.when(k == 0)
    def _():
        acc_ref[...] = jnp.zeros_like(acc_ref)

    acc_ref[...] += _dot(a_ref[...], b_ref[...])

    @pl.when(k == pl.num_programs(2) - 1)
    def _():
        rows = pl.program_id(0) * tm + lax.broadcasted_iota(jnp.int32, acc_ref.shape, 0)
        gate = jnp.where(rows < n_lat, gate_ref[0:1, :], gate_ref[1:2, :])
        o_ref[...] = res_ref[...] + coef * gate * acc_ref[...]


def _mm(a, b, *, mode="plain", out_dtype=F32, tn=None, tk=None, bias=None, res=None, gate=None,
        coef=1.0, n_lat=0):
    M, K = a.shape
    N = b.shape[1] // 2 if mode == "swiglu" else b.shape[1]
    tm = _pick(M, (1408, 1024, 768, 640, 512, 256, 128, 16))
    tn = tn or _pick(N, (1024, 768, 512, 384, 256, 128))
    tk = tk or _pick(K, (512, 256, 128))
    grid = (M // tm, N // tn, K // tk)
    a_spec = pl.BlockSpec((tm, tk), lambda i, j, k: (i, k))
    b_spec = pl.BlockSpec((tk, tn), lambda i, j, k: (k, j))
    o_spec = pl.BlockSpec((tm, tn), lambda i, j, k: (i, j))
    acc = pltpu.VMEM((tm, tn), F32)
    if mode == "plain":
        kern, in_specs, args, scratch = _mm_plain_kernel, [a_spec, b_spec], (a, b), [acc]
    elif mode == "bias":
        kern = _mm_bias_kernel
        in_specs = [a_spec, b_spec, pl.BlockSpec((1, tn), lambda i, j, k: (0, j))]
        args, scratch = (a, b, bias), [acc]
    elif mode == "swiglu":
        nj = N // tn
        kern = _mm_swiglu_kernel
        in_specs = [a_spec, b_spec, pl.BlockSpec((tk, tn), lambda i, j, k: (k, j + nj))]
        args, scratch = (a, b, b), [acc, acc]
    else:
        kern = functools.partial(_mm_resid_kernel, coef=coef, n_lat=n_lat, tm=tm)
        in_specs = [a_spec, b_spec, o_spec, pl.BlockSpec((2, tn), lambda i, j, k: (0, j))]
        args, scratch = (a, b, res, gate), [acc]
    return pl.pallas_call(
        kern,
        grid=grid,
        in_specs=in_specs,
        out_specs=o_spec,
        out_shape=jax.ShapeDtypeStruct((M, N), out_dtype),
        scratch_shapes=scratch,
        compiler_params=_cparams(("parallel", "parallel", "arbitrary")),
        name=f"mm_{mode}",
    )(*args)


def _rowwise(fn, toks, consts, outs, tm, *, halo=False, name="rowwise"):
    M = toks[0].shape[0]
    n_t, n_c = len(toks), len(consts)

    def kern(*refs):
        i = pl.program_id(0)
        vals = [r[...] for r in refs[:n_t + (2 if halo else 0) + n_c]]
        res = fn(i, *vals)
        for r, v in zip(refs[n_t + (2 if halo else 0) + n_c:], res):
            r[...] = v.astype(r.dtype)

    in_specs = [pl.BlockSpec((tm, t.shape[1]), lambda i: (i, 0)) for t in toks]
    args = list(toks)
    if halo:
        width = toks[0].shape[1]
        per, last = tm // SUBLANES, M // SUBLANES - 1
        in_specs.append(pl.BlockSpec((SUBLANES, width), lambda i: (jnp.maximum(i * per - 1, 0), 0)))
        in_specs.append(pl.BlockSpec((SUBLANES, width), lambda i: (jnp.minimum((i + 1) * per, last), 0)))
        args += [toks[0], toks[0]]
    for c in consts:
        in_specs.append(pl.BlockSpec(c.shape, lambda i, nd=c.ndim: (0,) * nd))
        args.append(c)
    return pl.pallas_call(
        kern,
        grid=(M // tm,),
        in_specs=in_specs,
        out_specs=[pl.BlockSpec((tm, w), lambda i: (i, 0)) for w, _ in outs],
        out_shape=[jax.ShapeDtypeStruct((M, w), dt) for w, dt in outs],
        compiler_params=_cparams(("parallel",)),
        name=name,
    )(*args)


def _row_ids(i, tm, shape):
    return i * tm + lax.broadcasted_iota(jnp.int32, shape, 0)


def _conv3(i, x, prev8, next8, w, tm, n_lat, n_all):
    rows = _row_ids(i, tm, x.shape)
    local = lax.broadcasted_iota(jnp.int32, x.shape, 0)
    prev_row = jnp.broadcast_to(prev8[SUBLANES - 1:SUBLANES, :], x.shape)
    next_row = jnp.broadcast_to(next8[0:1, :], x.shape)
    xp = jnp.where(local == 0, prev_row, pltpu.roll(x, 1, 0))
    xn = jnp.where(local == tm - 1, next_row, pltpu.roll(x, tm - 1, 0))
    xp = jnp.where((rows == 0) | (rows == n_lat), 0.0, xp)
    xn = jnp.where((rows == n_lat - 1) | (rows == n_all - 1), 0.0, xn)
    return xp * w[0:1, :] + x * w[1:2, :] + xn * w[2:3, :]


def _rms(x, gain, width):
    ms = jnp.sum(x * x, axis=-1, keepdims=True) * (1.0 / width)
    return x * lax.rsqrt(ms + NORM_EPS) * gain


def _wkv_kernel(rf, vf, kkf, wf, kf, bf, rb, vb, kkb, wb, kb, bb, eh_ref, dm_ref, hs_ref,
                yf_ref, yb_ref, s_ref, *, T):
    @pl.when(pl.program_id(0) == 0)
    def _():
        s_ref[...] = jnp.zeros_like(s_ref)

    yf_ref[...] = jnp.zeros_like(yf_ref)
    yb_ref[...] = jnp.zeros_like(yb_ref)
    eh = eh_ref[...]
    dm = dm_ref[...]
    hs = hs_ref[...]
    tile = (A_HEAD_DIM, WKV_W)
    dirs = ((rf, vf, kkf, wf, kf, bf, yf_ref), (rb, vb, kkb, wb, kb, bb, yb_ref))
    sub = lax.broadcasted_iota(jnp.int32, (SUBLANES, WKV_YW), 0)

    def bc(row, q):
        return jnp.broadcast_to(row[:, q * WKV_W:(q + 1) * WKV_W], tile)

    def step(t, carry):
        tts = (t, T - 1 - t)
        rows = []
        for d, (r_, v_, kk_, w_, k_, b_, _) in enumerate(dirs):
            tt = tts[d]
            rows.append(dict(r=r_[pl.ds(tt, 1), :], v=v_[pl.ds(tt, 1), :], kk=kk_[pl.ds(tt, 1), :],
                             w=w_[pl.ds(tt, 1), :], k=k_[pl.ds(tt, 1), :], b=b_[pl.ds(tt, 1), :]))
        nq = N_QUADS
        both = []
        for d in range(2):
            lhs = [(s_ref[d, q] * bc(rows[d]["kk"], q)).astype(BF16) for q in range(nq)]
            vrow = rows[d]["v"].astype(BF16)
            lhs += [bc(vrow, q) * dm for q in range(nq)]
            both.append(_dot(jnp.concatenate(lhs, axis=0), eh))
        ys = []
        for d in range(2):
            qs = []
            for q in range(nq):
                sa = both[d][q * A_HEAD_DIM:(q + 1) * A_HEAD_DIM]
                vcol = both[d][(nq + q) * A_HEAD_DIM:(nq + q + 1) * A_HEAD_DIM]
                s = (s_ref[d, q] * bc(rows[d]["w"], q) - sa * bc(rows[d]["b"], q)
                     + vcol * bc(rows[d]["k"], q))
                s_ref[d, q] = s
                qs.append((s * bc(rows[d]["r"], q)).astype(BF16))
            qs.append(jnp.zeros(tile, BF16))
            ys.append(lax.dot_general(hs, jnp.concatenate(qs, axis=0), (((1,), (1,)), ((), ())),
                                      preferred_element_type=F32))
        for d in range(2):
            y_, tt = dirs[d][6], tts[d]
            base = pl.multiple_of((tt // SUBLANES) * SUBLANES, SUBLANES)
            for h in range(WKV_W // A_HEAD_DIM):
                sl = slice(h * WKV_YW, (h + 1) * WKV_YW)
                old = y_[pl.ds(base, SUBLANES), sl]
                y_[pl.ds(base, SUBLANES), sl] = jnp.where(
                    sub == tt % SUBLANES, jnp.broadcast_to(ys[d][h:h + 1, :], old.shape), old)
        return carry

    lax.fori_loop(0, T, step, 0, unroll=WKV_UNROLL)


def _wkv(r, v, kk, w0, k0, b0, w1, k1, b1, eh, dm, hs, n_lat):
    M = r.shape[0]
    T = WKV_T
    nx, nb = n_lat // T, M // T
    nc = nb - nx

    def fwd(i):
        return (jnp.where(i < nc, nx + i, i - nc), 0)

    def bwd(i):
        return (jnp.where(i < nc, nx + (nc - 1 - i), nx - 1 - (i - nc)), 0)

    fspec = pl.BlockSpec((T, A_WIDTH), fwd)
    bspec = pl.BlockSpec((T, A_WIDTH), bwd)
    hpq = WKV_W // A_HEAD_DIM
    yw = hpq * WKV_YW
    ys = pl.pallas_call(
        functools.partial(_wkv_kernel, T=T),
        grid=(nb,),
        in_specs=[fspec] * 6 + [bspec] * 6 + [
            pl.BlockSpec(c.shape, lambda i: (0, 0)) for c in (eh, dm, hs)],
        out_specs=[pl.BlockSpec((T, yw), fwd), pl.BlockSpec((T, yw), bwd)],
        out_shape=[jax.ShapeDtypeStruct((M, yw), F32)] * 2,
        scratch_shapes=[pltpu.VMEM((2, N_QUADS, A_HEAD_DIM, WKV_W), F32)],
        compiler_params=_cparams(("arbitrary",)),
        name="wkv7_scan",
    )(r, v, kk, w0, k0, b0, r, v, kk, w1, k1, b1, eh, dm, hs)
    nqp = WKV_YW // A_HEAD_DIM
    return [y.reshape(M, hpq, nqp, A_HEAD_DIM)[:, :, :N_QUADS].transpose(0, 2, 1, 3).reshape(M, A_WIDTH)
            for y in ys]


def _dft_rows_kernel(fc_ref, fs_ref, x_ref, ar_ref, ai_ref):
    x = x_ref[...]
    ar_ref[...] = _dot3(fc_ref[...], x)
    ai_ref[...] = -_dot3(fs_ref[...], x)


def _dft_rows(fc, fs, x2d):
    n1h, W = x2d.shape
    n1 = fc.shape[0]
    tn = _pick(W, (4096, 2048, 1024, 512, 256, 128))
    fspec = pl.BlockSpec(fc.shape, lambda j: (0, 0))
    return pl.pallas_call(
        _dft_rows_kernel,
        grid=(W // tn,),
        in_specs=[fspec, fspec, pl.BlockSpec((n1h, tn), lambda j: (0, j))],
        out_specs=[pl.BlockSpec((n1, tn), lambda j: (0, j))] * 2,
        out_shape=[jax.ShapeDtypeStruct((n1, W), F32)] * 2,
        compiler_params=_cparams(("parallel",)),
        name="fft_stage1",
    )(fc, fs, x2d)


def _tile_lanes(t, width):
    return t if width == LANES else jnp.concatenate([t] * (width // LANES), axis=-1)


def _cdot(m3, xr, xi):
    x = jnp.concatenate([xr, xi], axis=0)
    hi, lo = _split(x)
    out = _dot(m3, jnp.concatenate([hi, hi, lo], axis=0))
    return out[:FFT_N2], out[FFT_N2:]


def _fft_fwd_mid(ar, ai, twr, twi, mf):
    return _cdot(mf, ar * twr - ai * twi, ar * twi + ai * twr)


def _filter_spec_kernel(afr, afi, apr, api, twr_ref, twi_ref, mf_ref, hr_ref, hi_ref):
    ct = afr.shape[-1]
    twr, twi = _tile_lanes(twr_ref[0], ct), _tile_lanes(twi_ref[0], ct)
    mf = mf_ref[...]
    fr, fi = _fft_fwd_mid(afr[0], afi[0], twr, twi, mf)
    pr, pi = _fft_fwd_mid(apr[0], api[0], twr, twi, mf)
    hr_ref[0] = fr + pr
    hi_ref[0] = fi - pi


def _conv_spec_kernel(agr, agi, hr, hi, twr_ref, twi_ref, mf_ref, mi_ref, dr_ref, di_ref):
    ct = agr.shape[-1]
    twr, twi = _tile_lanes(twr_ref[0], ct), _tile_lanes(twi_ref[0], ct)
    xr, xi = _fft_fwd_mid(agr[0], agi[0], twr, twi, mf_ref[...])
    yr = xr * hr[0] - xi * hi[0]
    yi = xr * hi[0] + xi * hr[0]
    cr, ci = _cdot(mi_ref[...], yr, yi)
    dr_ref[0] = cr * twr + ci * twi
    di_ref[0] = ci * twr - cr * twi


def _idft_rows_kernel(gc_ref, gs_ref, dr_ref, di_ref, y_ref, *, scale):
    y_ref[...] = scale * (_dot3(gc_ref[...], dr_ref[...]) - _dot3(gs_ref[...], di_ref[...]))


def _idft_rows(gc, gs, dr2d, di2d, scale):
    n1, W = dr2d.shape
    n1h = gc.shape[0]
    tn = _pick(W, (4096, 2048, 1024, 512, 256, 128))
    gspec = pl.BlockSpec(gc.shape, lambda j: (0, 0))
    dspec = pl.BlockSpec((n1, tn), lambda j: (0, j))
    return pl.pallas_call(
        functools.partial(_idft_rows_kernel, scale=scale),
        grid=(W // tn,),
        in_specs=[gspec, gspec, dspec, dspec],
        out_specs=pl.BlockSpec((n1h, tn), lambda j: (0, j)),
        out_shape=jax.ShapeDtypeStruct((n1h, W), F32),
        compiler_params=_cparams(("parallel",)),
        name="fft_stage1_inv",
    )(gc, gs, dr2d, di2d)


def _fft_tables(n_lat):
    n = 2 * n_lat
    n1 = n // FFT_N2

    def cs(rows, cols, period):
        idx = (jnp.arange(rows, dtype=jnp.int32)[:, None] * jnp.arange(cols, dtype=jnp.int32)[None, :]) % period
        ang = idx.astype(F32) * (2.0 * math.pi / period)
        return jnp.cos(ang), jnp.sin(ang)

    f1c, f1s = cs(n1, n1, n1)
    f2c, f2s = cs(FFT_N2, FFT_N2, FFT_N2)
    twc, tws = cs(n1, FFT_N2, n)
    shape = (n1, FFT_N2, LANES)
    twc = jnp.broadcast_to(twc[:, :, None], shape)
    tws = jnp.broadcast_to(tws[:, :, None], shape)

    def three_pass(m):
        hi = m.astype(BF16)
        lo = (m - hi.astype(F32)).astype(BF16)
        return jnp.concatenate([hi, lo, hi], axis=1)

    mf = three_pass(jnp.block([[f2c, f2s], [-f2s, f2c]]))
    mi = three_pass(jnp.block([[f2c, -f2s], [f2s, f2c]]))
    return dict(n=n, n1=n1, f1c=f1c, f1s=f1s, mf=mf, mi=mi, twr=twc, twi=-tws)


def _long_conv(g_seqs, h_seqs, tabs):
    n_lat, C = g_seqs[0].shape
    n1, n = tabs["n1"], tabs["n"]
    ns = len(g_seqs)
    f1c_h, f1s_h = tabs["f1c"][:, :n1 // 2], tabs["f1s"][:, :n1 // 2]
    hf = jnp.concatenate([h[0] for h in h_seqs], axis=1)
    hp = jnp.concatenate([h[1] for h in h_seqs], axis=1)
    q2d = jnp.concatenate([hf, hp], axis=1).reshape(n1 // 2, FFT_N2 * 2 * ns * C)
    aqr, aqi = _dft_rows(f1c_h, f1s_h, q2d)
    aqr = aqr.reshape(n1, FFT_N2, 2 * ns * C)
    aqi = aqi.reshape(n1, FFT_N2, 2 * ns * C)
    ct = _pick(ns * C, (512, 256, 128))
    nj = ns * C // ct
    blk_f = pl.BlockSpec((1, FFT_N2, ct), lambda k, j: (k, 0, j))
    blk_p = pl.BlockSpec((1, FFT_N2, ct), lambda k, j: (k, 0, j + nj))
    tw = pl.BlockSpec((1, FFT_N2, LANES), lambda k, j: (k, 0, 0))
    fm = pl.BlockSpec(tabs["mf"].shape, lambda k, j: (0, 0))
    hr, hi = pl.pallas_call(
        _filter_spec_kernel,
        grid=(n1, nj),
        in_specs=[blk_f, blk_f, blk_p, blk_p, tw, tw, fm],
        out_specs=[blk_f, blk_f],
        out_shape=[jax.ShapeDtypeStruct((n1, FFT_N2, ns * C), F32)] * 2,
        compiler_params=_cparams(("parallel", "parallel")),
        name="fft_filter_spectrum",
    )(aqr, aqi, aqr, aqi, tabs["twr"], tabs["twi"], tabs["mf"])
    g2d = jnp.concatenate(g_seqs, axis=1).reshape(n1 // 2, FFT_N2 * ns * C)
    agr, agi = _dft_rows(f1c_h, f1s_h, g2d)
    agr = agr.reshape(n1, FFT_N2, ns * C)
    agi = agi.reshape(n1, FFT_N2, ns * C)
    dr, di = pl.pallas_call(
        _conv_spec_kernel,
        grid=(n1, nj),
        in_specs=[blk_f] * 4 + [tw, tw, fm, fm],
        out_specs=[blk_f, blk_f],
        out_shape=[jax.ShapeDtypeStruct((n1, FFT_N2, ns * C), F32)] * 2,
        compiler_params=_cparams(("parallel", "parallel")),
        name="fft_conv_spectrum",
    )(agr, agi, hr, hi, tabs["twr"], tabs["twi"], tabs["mf"], tabs["mi"])
    y2d = _idft_rows(tabs["f1c"][:n1 // 2, :], tabs["f1s"][:n1 // 2, :],
                     dr.reshape(n1, FFT_N2 * ns * C), di.reshape(n1, FFT_N2 * ns * C), 1.0 / n)
    y = y2d.reshape(n_lat, ns * C)
    return [y[:, i * C:(i + 1) * C] for i in range(ns)]


def _filter_feats(length):
    t = jnp.linspace(0.0, 1.0, length, dtype=F32)[:, None]
    w = (2.0 * math.pi / length) * jnp.arange(length, dtype=F32)[:, None]
    f = jnp.linspace(1e-4, FILTER_BANDS - 1, FILTER_BANDS, dtype=F32)[None, :]
    z = jnp.concatenate([t, jnp.cos(f * w), -jnp.sin(f * w)], axis=-1)
    return jnp.pad(z, ((0, 0), (0, LANES - FILTER_EMB)))


def _filter_kernel(z_ref, w1, b1, w2, b2, w3, b3, w4, fr, dl, h_ref, s_ref, *, tm, length):
    i = pl.program_id(0)
    z = z_ref[...]
    freq = fr[...]
    h = jnp.sin(freq * (_dot3(z, w1[...]) + b1[...]))
    h = jnp.sin(freq * (_dot3(h, w2[...]) + b2[...]))
    h = jnp.sin(freq * (_dot3(h, w3[...]) + b3[...]))
    h = _dot3(h, w4[...])
    h = h * jnp.exp(-z[:, 0:1] * dl[...])
    h_ref[...] = h
    rows = _row_ids(i, tm, h.shape)
    cols = lax.broadcasted_iota(jnp.int32, h.shape, 1)
    keep = (cols < B_WIDTH) | (rows < length - 1)
    part = jnp.sum(jnp.where(keep, jnp.abs(h), 0.0), axis=0, keepdims=True)

    @pl.when(i == 0)
    def _():
        s_ref[...] = jnp.zeros_like(s_ref)

    s_ref[...] += jnp.broadcast_to(part, s_ref.shape)


def _filter_taps(length, fp):
    z = _filter_feats(length)
    tm = _pick(length, (256, 128))
    consts = [fp["w1"], fp["b1"], fp["w2"], fp["b2"], fp["w3"], fp["b3"], fp["w4"], fp["freq"], fp["deltas"]]
    in_specs = [pl.BlockSpec((tm, LANES), lambda i: (i, 0))]
    in_specs += [pl.BlockSpec(c.shape, lambda i: (0, 0)) for c in consts]
    h, s = pl.pallas_call(
        functools.partial(_filter_kernel, tm=tm, length=length),
        grid=(length // tm,),
        in_specs=in_specs,
        out_specs=[pl.BlockSpec((tm, 2 * B_WIDTH), lambda i: (i, 0)),
                   pl.BlockSpec((SUBLANES, 2 * B_WIDTH), lambda i: (0, 0))],
        out_shape=[jax.ShapeDtypeStruct((length, 2 * B_WIDTH), F32),
                   jax.ShapeDtypeStruct((SUBLANES, 2 * B_WIDTH), F32)],
        compiler_params=_cparams(("arbitrary",)),
        name="hyena_filter",
    )(z, *consts)
    return h, s


def _attn_kernel(q_ref, k_ref, v_ref, o_ref, m_ref, l_ref, acc_ref):
    j = pl.program_id(2)

    @pl.when(j == 0)
    def _():
        m_ref[...] = jnp.full_like(m_ref, -jnp.inf)
        l_ref[...] = jnp.zeros_like(l_ref)
        acc_ref[...] = jnp.zeros_like(acc_ref)

    k, v = k_ref[...], v_ref[...]
    for r0 in range(0, q_ref.shape[0], ATTN_SUB):
        rs = slice(r0, r0 + ATTN_SUB)
        s = lax.dot_general(q_ref[rs, :], k, (((1,), (1,)), ((), ())), preferred_element_type=F32)
        m_old = m_ref[rs, :]
        m_new = jnp.maximum(m_old, jnp.max(s, axis=-1, keepdims=True))
        alpha = jnp.exp(m_old - m_new)
        p = jnp.exp(s - m_new)
        l_ref[rs, :] = alpha * l_ref[rs, :] + jnp.sum(p, axis=-1, keepdims=True)
        acc_ref[rs, :] = alpha * acc_ref[rs, :] + _dot(p.astype(BF16), v)
        m_ref[rs, :] = m_new

    @pl.when(j == pl.num_programs(2) - 1)
    def _():
        o_ref[...] = (acc_ref[...] / l_ref[...]).astype(o_ref.dtype)


def _attention(q, k, v, q_row0, n_q, k_row0, n_k):
    tq = _pick(n_q, (1024, 512, 256))
    tk = _pick(n_k, (768, 1024, 640, 512, 256))
    assert q_row0 % tq == 0 and k_row0 % tk == 0
    qb, kb = q_row0 // tq, k_row0 // tk
    return pl.pallas_call(
        _attn_kernel,
        grid=(C_HEADS, n_q // tq, n_k // tk),
        in_specs=[pl.BlockSpec((tq, QH), lambda h, i, j: (qb + i, h)),
                  pl.BlockSpec((tk, QH), lambda h, i, j: (kb + j, h)),
                  pl.BlockSpec((tk, V_HEAD), lambda h, i, j: (kb + j, h))],
        out_specs=pl.BlockSpec((tq, V_HEAD), lambda h, i, j: (i, h)),
        out_shape=jax.ShapeDtypeStruct((n_q, C_WIDTH), BF16),
        scratch_shapes=[pltpu.VMEM((tq, 1), F32), pltpu.VMEM((tq, 1), F32), pltpu.VMEM((tq, V_HEAD), F32)],
        compiler_params=_cparams(("parallel", "parallel", "arbitrary")),
        name="mla_attention",
    )(q, k, v)


def _pair_sum(x, eh):
    return jnp.concatenate(
        [_dot2(x[:, p * LANES:(p + 1) * LANES], eh) for p in range(N_PAIRS)], axis=-1)


def _rot_half(t):
    lane = lax.broadcasted_iota(jnp.int32, t.shape, 1)
    up = pltpu.roll(t, LANES - 16, 1)
    down = pltpu.roll(t, 16, 1)
    return jnp.where(lane % 32 < 16, -up, down)


def _layer(xa, mod, p, const, n_lat, need_ctx):
    M = xa.shape[0]
    eh = const["eh"]

    def modrow(k):
        return mod[:, k * D_MODEL:(k + 1) * D_MODEL]

    def norm_mod(x, gain, shift, scale):
        tm = 256

        def fn(i, xb, g, sh, sc):
            rows = _row_ids(i, tm, xb.shape)
            lat = rows < n_lat
            y = _rms(xb, g, D_MODEL)
            return (y * (1.0 + jnp.where(lat, sc[0:1], sc[1:2])) + jnp.where(lat, sh[0:1], sh[1:2]),)

        return _rowwise(fn, [x], [gain, shift, scale], [(D_MODEL, BF16)], tm, name="norm_modulate")[0]

    def ffn(x, gain, w_gu, w_down, k0):
        h = norm_mod(x, gain, modrow(k0), modrow(k0 + 1))
        act = _mm(h, w_gu, mode="swiglu", out_dtype=BF16, tn=1024)
        return _mm(act, w_down, mode="resid", res=x, gate=modrow(k0 + 2), coef=0.5, n_lat=n_lat)

    xa = ffn(xa, p["norm_ffn1"], p["ffn1_gu"], p["ffn1_down"], 0)

    h = norm_mod(xa, p["norm_mix"], modrow(3), modrow(4))
    pa = _mm(h, p["w_in_a"], tn=1408)
    pb = _mm(h, p["w_in_b"], tn=1280)
    pc = _mm(h, p["w_in_c"], tn=C_PAD)

    tma = 128

    def rwkv_prep(i, x, prev8, next8, cw, k_k, k_a, w0, wup0, wup1, a0, aup0, aup1, gup, eh_):
        u = _conv3(i, x, prev8, next8, cw, tma, n_lat, M)
        r, k, v = u[:, :A_WIDTH], u[:, A_WIDTH:2 * A_WIDTH], u[:, 2 * A_WIDTH:3 * A_WIDTH]
        xwa = u[:, 3 * A_WIDTH:3 * A_WIDTH + LANES]
        xg = u[:, 3 * A_WIDTH + LANES:3 * A_WIDTH + 3 * LANES]
        kk = k * k_k
        kk = kk / jnp.maximum(jnp.sqrt(_pair_sum(kk * kk, eh_)), 1e-12)
        g = _dot3(_sigmoid(xg), gup)
        th = jnp.tanh(xwa)
        outs = [r, v, kk, g]
        for wup, aup, d in ((wup0, aup0, 0), (wup1, aup1, 1)):
            logw = -_softplus(-(w0[d:d + 1] + _dot3(th, wup))) - 0.5
            decay = jnp.exp(-jnp.exp(logw))
            a = _sigmoid(a0[d:d + 1] + _dot3(xwa, aup))
            outs += [decay, k * (1.0 + (a - 1.0) * k_a), kk * a]
        return outs

    r, v, kk, g, w0, k0, b0, w1, k1, b1 = _rowwise(
        rwkv_prep, [pa],
        [p["a_conv"], p["a_k_k"], p["a_k_a"], p["a_w0"], p["a_w_up0"], p["a_w_up1"], p["a_a0"],
         p["a_a_up0"], p["a_a_up1"], p["a_g_up"], eh],
        [(A_WIDTH, F32)] * 10, tma, halo=True, name="rwkv_prep")
    yf, yb = _wkv(r, v, kk, w0, k0, b0, w1, k1, b1, const["eh2"], const["dm2"], const["hs"], n_lat)

    def rwkv_out(i, yf_, yb_, r_, v_, g_, k0_, k1_, r_k, ln_w, ln_b, eh_):
        y = yf_ + yb_
        inv = 1.0 / A_HEAD_DIM
        mu = _pair_sum(y, eh_) * inv
        yc = y - mu
        var = _pair_sum(yc * yc, eh_) * inv
        yn = yc * lax.rsqrt(var + GN_EPS) * ln_w + ln_b
        bonus = _pair_sum(r_ * (0.5 * (k0_ + k1_)) * r_k, eh_) * v_
        return ((yn + bonus) * g_,)

    o_a = _rowwise(rwkv_out, [yf, yb, r, v, g, k0, k1], [p["a_r_k"], p["a_ln_w"], p["a_ln_b"], eh],
                   [(A_WIDTH, BF16)], 256, name="rwkv_out")[0]

    tmb = 256

    def hyena_prep(i, x, prev8, next8, cw, cb):
        u = _conv3(i, x, prev8, next8, cw, tmb, n_lat, M) + cb
        return u[:, :B_WIDTH], u[:, B_WIDTH:2 * B_WIDTH] * u[:, 2 * B_WIDTH:]

    x0, gsig = _rowwise(hyena_prep, [pb], [p["b_conv"], p["b_conv_b"]],
                        [(B_WIDTH, F32)] * 2, tmb, halo=True, name="hyena_prep")
    n_ctx = M - n_lat
    hx, sx = _filter_taps(n_lat, p["filt"])
    hc, sc = _filter_taps(n_ctx, p["filt"])

    def shifted(hb):
        return jnp.pad(hb[:-1], ((1, 0), (0, 0)))

    def padrows(t):
        return jnp.pad(t, ((0, n_lat - t.shape[0]), (0, 0)))

    ys = _long_conv(
        [gsig[:n_lat], padrows(gsig[n_lat:])],
        [(hx[:, :B_WIDTH], shifted(hx[:, B_WIDTH:])),
         (padrows(hc[:, :B_WIDTH]), padrows(shifted(hc[:, B_WIDTH:])))],
        const["fft"])
    yconv = jnp.concatenate([ys[0], ys[1][:n_ctx]], axis=0)

    def hyena_out(i, x0_, g_, y_, bias, sx_, sc_):
        rows = _row_ids(i, tmb, x0_.shape)
        nx_ = sx_[0:1, :B_WIDTH] + sx_[0:1, B_WIDTH:]
        nc_ = sc_[0:1, :B_WIDTH] + sc_[0:1, B_WIDTH:]
        norm = jnp.where(rows < n_lat, nx_, nc_)
        return (x0_ * (y_ / norm + g_ * bias),)

    o_b = _rowwise(hyena_out, [x0, gsig, yconv], [p["b_bias"], sx, sc], [(B_WIDTH, BF16)], tmb,
                   name="hyena_out")[0]

    tmc = 256

    def mla_prep(i, x, cos, sin, gq, gkv, gpe):
        cq, ckv = x[:, :Q_RANK], x[:, Q_RANK:Q_RANK + KV_RANK]
        kpe = x[:, Q_RANK + KV_RANK:]
        kpe = _rms(kpe, gpe, QK_ROPE)
        kpe = kpe * cos + _rot_half(kpe) * sin
        return _rms(cq, gq, Q_RANK), _rms(ckv, gkv, KV_RANK), kpe

    cqn, ckvn, kpe = _rowwise(mla_prep, [pc, const["cos"], const["sin"]],
                              [p["c_q_norm"], p["c_kv_norm"], p["c_kn_pe"]],
                              [(Q_RANK, BF16), (KV_RANK, BF16), (LANES, F32)], tmc, name="mla_prep")
    q_raw = _mm(cqn, p["c_q_up"], tn=768)
    kv_raw = _mm(ckvn, p["c_kv_up"], tn=768)
    scale = 1.0 / math.sqrt(QK_DIM)

    def mla_heads(i, qr, kvr, kpe_, cos, sin, gn_q, gpe_q, gn_k):
        qs, ks, vs = [], [], []
        for hh in range(C_HEADS):
            qn = _rms(qr[:, hh * QH:hh * QH + QK_NOPE], gn_q, QK_NOPE)
            qp = _rms(qr[:, hh * QH + QK_NOPE:(hh + 1) * QH], gpe_q, QK_ROPE)
            qp = qp * cos + _rot_half(qp) * sin
            qs += [qn * scale, qp * scale]
            ks += [_rms(kvr[:, hh * QH:hh * QH + QK_NOPE], gn_k, QK_NOPE), kpe_]
            vs.append(kvr[:, hh * QH + QK_NOPE:(hh + 1) * QH])
        return jnp.concatenate(qs, axis=-1), jnp.concatenate(ks, axis=-1), jnp.concatenate(vs, axis=-1)

    q, k, vv = _rowwise(mla_heads, [q_raw, kv_raw, kpe, const["cos"], const["sin"]],
                        [p["c_qn_nope"], p["c_qn_pe"], p["c_kn_nope"]],
                        [(C_HEADS * QH, BF16), (C_HEADS * QH, BF16), (C_WIDTH, BF16)], tmc, name="mla_heads")
    oc_x = _attention(q, k, vv, 0, n_lat, 0, M)
    if need_ctx:
        oc_c = _attention(q, k, vv, n_lat, n_ctx, n_lat, n_ctx)
    else:
        oc_c = jnp.zeros((n_ctx, C_WIDTH), BF16)
    o_c = jnp.concatenate([oc_x, oc_c], axis=0)

    o = jnp.concatenate([o_a, o_b, o_c], axis=-1)
    xa = _mm(o, p["w_out"], mode="resid", res=xa, gate=modrow(5), coef=1.0, n_lat=n_lat)
    return ffn(xa, p["norm_ffn2"], p["ffn2_gu"], p["ffn2_down"], 6)


def _pad_cols(w, width):
    return jnp.pad(w, [(0, 0)] * (w.ndim - 1) + [(0, width - w.shape[-1])])


def _pad_rows(w, height):
    return jnp.pad(w, [(0, 0)] * (w.ndim - 2) + [(0, height - w.shape[-2]), (0, 0)])


def _row(v):
    return v.reshape(1, -1)


def kernel(x, c, ctx, c_ctx, ada_down, ada_up, ada_bias, norm_ffn1, norm_mix, norm_ffn2, ffn1_gu, ffn1_down, ffn2_gu, ffn2_down, w_in, w_out, a_conv, a_w0, a_w_up, a_a0, a_a_up, a_g_up, a_k_k, a_k_a, a_r_k, a_ln_w, a_ln_b, b_conv, b_conv_b, b_fw1, b_fb1, b_fw2, b_fb2, b_fw3, b_fb3, b_fw4, b_freq, b_bias, c_q_norm, c_q_up, c_kv_norm, c_kv_up, c_qn_nope, c_qn_pe, c_kn_nope, c_kn_pe):
    n_lat, n_ctx = x.shape[1], ctx.shape[1]
    depth = w_in.shape[0]
    xa = jnp.concatenate([x[0], ctx[0]], axis=0)

    lane = jnp.arange(LANES)
    eh = (lane[:, None] // A_HEAD_DIM == lane[None, :] // A_HEAD_DIM).astype(BF16)
    lane2 = jnp.arange(WKV_W)
    eh2 = (lane2[:, None] // A_HEAD_DIM == lane2[None, :] // A_HEAD_DIM).astype(BF16)
    dm2 = (lane2[None, :] % A_HEAD_DIM == jnp.arange(A_HEAD_DIM)[:, None]).astype(BF16)
    rows_g = n_lat // GRID_W
    row = jnp.broadcast_to(jnp.arange(rows_g, dtype=F32)[:, None], (rows_g, GRID_W)).reshape(-1)
    col = jnp.broadcast_to(jnp.arange(GRID_W, dtype=F32)[None, :], (rows_g, GRID_W)).reshape(-1)
    half = QK_ROPE // 2
    inv = ROPE_THETA ** (-jnp.arange(0, half, 2, dtype=F32) / half)
    ang = jnp.concatenate([row[:, None] * inv, row[:, None] * inv, col[:, None] * inv, col[:, None] * inv], axis=-1)
    cos = jnp.concatenate([jnp.cos(ang), jnp.ones((n_ctx, QK_ROPE), F32)], axis=0)
    sin = jnp.concatenate([jnp.sin(ang), jnp.zeros((n_ctx, QK_ROPE), F32)], axis=0)
    hs = (lane2[None, :] // A_HEAD_DIM == jnp.arange(16)[:, None]).astype(BF16)
    const = dict(eh=eh, eh2=eh2, dm2=dm2, hs=hs, cos=_pad_cols(cos, LANES), sin=_pad_cols(sin, LANES), fft=_fft_tables(n_lat))
    deltas = jnp.abs(jnp.linspace(MIN_DECAY, MAX_DECAY, B_WIDTH, dtype=F32))

    cc = jnp.pad(jnp.concatenate([c, c_ctx[None, :]], axis=0), ((0, 14), (0, 0)))
    cs = _rowwise(lambda i, t: (_silu(t),), [cc], [], [(D_MODEL, BF16)], 16, name="silu")[0]

    q_up = c_q_up.reshape(depth, Q_RANK, C_HEADS, QK_DIM)
    q_up = jnp.pad(q_up, ((0, 0), (0, 0), (0, 0), (0, QH - QK_DIM))).reshape(depth, Q_RANK, C_HEADS * QH)

    for l in range(depth):
        mid = _mm(cs, ada_down[l].astype(BF16), out_dtype=BF16, tn=256)
        mod = _mm(mid, ada_up[l].astype(BF16), mode="bias", bias=_row(ada_bias[l]), tn=1024, tk=256)[:2]
        w_in_l = w_in[l]
        filt = dict(
            w1=_pad_cols(_pad_rows(b_fw1[l], LANES), LANES), b1=_pad_cols(_row(b_fb1[l]), LANES),
            w2=_pad_cols(_pad_rows(b_fw2[l], LANES), LANES), b2=_pad_cols(_row(b_fb2[l]), LANES),
            w3=_pad_cols(_pad_rows(b_fw3[l], LANES), LANES), b3=_pad_cols(_row(b_fb3[l]), LANES),
            w4=_pad_rows(b_fw4[l], LANES), freq=_pad_cols(_row(b_freq[l]), LANES),
            deltas=_row(jnp.tile(deltas, 2)))
        zpad = jnp.zeros((A_DECAY_RANK, A_WIDTH), F32)
        p = dict(
            norm_ffn1=_row(norm_ffn1[l]), norm_mix=_row(norm_mix[l]), norm_ffn2=_row(norm_ffn2[l]),
            ffn1_gu=ffn1_gu[l].astype(BF16), ffn1_down=ffn1_down[l].astype(BF16),
            ffn2_gu=ffn2_gu[l].astype(BF16), ffn2_down=ffn2_down[l].astype(BF16),
            w_in_a=_pad_cols(w_in_l[:, :A_IN], A_PAD).astype(BF16),
            w_in_b=w_in_l[:, A_IN:A_IN + B_IN].astype(BF16),
            w_in_c=_pad_cols(w_in_l[:, A_IN + B_IN:], C_PAD).astype(BF16),
            w_out=w_out[l].astype(BF16),
            a_conv=_pad_cols(a_conv[l], A_PAD), a_k_k=_row(a_k_k[l]), a_k_a=_row(a_k_a[l]),
            a_w0=a_w0[l], a_a0=a_a0[l],
            a_w_up0=jnp.concatenate([a_w_up[l, 0], zpad], axis=0),
            a_w_up1=jnp.concatenate([a_w_up[l, 1], zpad], axis=0),
            a_a_up0=jnp.concatenate([zpad, a_a_up[l, 0]], axis=0),
            a_a_up1=jnp.concatenate([zpad, a_a_up[l, 1]], axis=0),
            a_g_up=_pad_rows(a_g_up[l], 2 * LANES),
            a_r_k=_row(a_r_k[l]), a_ln_w=_row(a_ln_w[l]), a_ln_b=_row(a_ln_b[l]),
            b_conv=b_conv[l], b_conv_b=_row(b_conv_b[l]), b_bias=_row(b_bias[l]), filt=filt,
            c_q_norm=_row(c_q_norm[l]), c_kv_norm=_row(c_kv_norm[l]),
            c_kn_pe=_pad_cols(_row(c_kn_pe[l]), LANES),
            c_q_up=q_up[l].astype(BF16), c_kv_up=c_kv_up[l].astype(BF16),
            c_qn_nope=_row(c_qn_nope[l]), c_qn_pe=_pad_cols(_row(c_qn_pe[l]), LANES),
            c_kn_nope=_row(c_kn_nope[l]),
        )
        xa = _layer(xa, mod, p, const, n_lat, l < depth - 1)
    return xa[:n_lat][None]
```

```python
import functools
import math

import jax
import jax.numpy as jnp
from jax import lax
from jax.experimental import pallas as pl
from jax.experimental.pallas import tpu as pltpu

F32 = jnp.float32
BF16 = jnp.bfloat16

D_MODEL = 4096
GRID_W = 64
A_HEADS = 20
A_HEAD_DIM = 64
A_WIDTH = A_HEADS * A_HEAD_DIM
A_DECAY_RANK = 64
A_ICLR_RANK = 64
A_GATE_RANK = 192
A_IN = 3 * A_WIDTH + A_DECAY_RANK + A_ICLR_RANK + A_GATE_RANK
GN_EPS = 64e-5
B_WIDTH = 1280
B_IN = 3 * B_WIDTH
FILTER_EMB = 33
FILTER_BANDS = (FILTER_EMB - 1) // 2
FILTER_HIDDEN = 64
MIN_DECAY = math.log(1e-2) / 1.5
MAX_DECAY = math.log(1e-2) / 0.3
C_HEADS = 12
QK_NOPE = 128
QK_ROPE = 64
QK_DIM = QK_NOPE + QK_ROPE
V_HEAD = 128
C_WIDTH = C_HEADS * V_HEAD
Q_RANK = 1024
KV_RANK = 512
C_IN = Q_RANK + KV_RANK + QK_ROPE
ROPE_THETA = 10000.0
FFN_HIDDEN = 6144
N_MOD = 9
NORM_EPS = 1e-6

LANES = 128
SUBLANES = 8
VMEM_LIMIT = 56 * 1024 * 1024

A_PAD = 4224
C_PAD = 1664
QH = 256
N_PAIRS = A_HEADS // 2
WKV_W = 256
N_QUADS = A_WIDTH // WKV_W
WKV_YW = 384
WKV_T = 128
FFT_N2 = 128
ATTN_SUB = 256


def _cparams(sem):
    return pltpu.CompilerParams(dimension_semantics=sem, vmem_limit_bytes=VMEM_LIMIT)


def _pick(n, cands):
    for c in cands:
        if n % c == 0:
            return c
    raise ValueError(f"no tile for {n}")


def _dot(a, b):
    return jnp.dot(a, b, preferred_element_type=F32)


def _split(a):
    hi = a.astype(BF16)
    lo = (a - hi.astype(F32)).astype(BF16)
    return hi, lo


def _dot3(a, b):
    ah, al = _split(a)
    bh, bl = _split(b)
    return _dot(ah, bh) + _dot(al, bh) + _dot(ah, bl)


def _dot2(a, b_bf16):
    ah, al = _split(a)
    return _dot(ah, b_bf16) + _dot(al, b_bf16)


def _sigmoid(x):
    return 1.0 / (1.0 + jnp.exp(-x))


def _silu(x):
    return x * _sigmoid(x)


def _softplus(x):
    return jnp.maximum(x, 0.0) + jnp.log(1.0 + jnp.exp(-jnp.abs(x)))


def _mm_plain_kernel(a_ref, b_ref, o_ref, acc_ref):
    k = pl.program_id(2)

    @pl.when(k == 0)
    def _():
        acc_ref[...] = jnp.zeros_like(acc_ref)

    acc_ref[...] += _dot(a_ref[...], b_ref[...])

    @pl.when(k == pl.num_programs(2) - 1)
    def _():
        o_ref[...] = acc_ref[...].astype(o_ref.dtype)


def _mm_bias_kernel(a_ref, b_ref, bias_ref, o_ref, acc_ref):
    k = pl.program_id(2)

    @pl.when(k == 0)
    def _():
        acc_ref[...] = jnp.zeros_like(acc_ref)

    acc_ref[...] += _dot(a_ref[...], b_ref[...])

    @pl.when(k == pl.num_programs(2) - 1)
    def _():
        o_ref[...] = (acc_ref[...] + bias_ref[...]).astype(o_ref.dtype)


def _mm_swiglu_kernel(a_ref, bg_ref, bu_ref, o_ref, accg_ref, accu_ref):
    k = pl.program_id(2)

    @pl.when(k == 0)
    def _():
        accg_ref[...] = jnp.zeros_like(accg_ref)
        accu_ref[...] = jnp.zeros_like(accu_ref)

    a = a_ref[...]
    accg_ref[...] += _dot(a, bg_ref[...])
    accu_ref[...] += _dot(a, bu_ref[...])

    @pl.when(k == pl.num_programs(2) - 1)
    def _():
        o_ref[...] = (_silu(accg_ref[...]) * accu_ref[...]).astype(o_ref.dtype)


def _mm_resid_kernel(a_ref, b_ref, res_ref, gate_ref, o_ref, acc_ref, *, coef, n_lat, tm):
    k = pl.program_id(2)

    @pl.when(k == 0)
    def _():
        acc_ref[...] = jnp.zeros_like(acc_ref)

    acc_ref[...] += _dot(a_ref[...], b_ref[...])

    @pl.when(k == pl.num_programs(2) - 1)
    def _():
        rows = pl.program_id(0) * tm + lax.broadcasted_iota(jnp.int32, acc_ref.shape, 0)
        gate = jnp.where(rows < n_lat, gate_ref[0:1, :], gate_ref[1:2, :])
        o_ref[...] = res_ref[...] + coef * gate * acc_ref[...]


def _mm(a, b, *, mode="plain", out_dtype=F32, tn=None, tk=None, bias=None, res=None, gate=None,
        coef=1.0, n_lat=0):
    M, K = a.shape
    N = b.shape[1] // 2 if mode == "swiglu" else b.shape[1]
    tm = _pick(M, (1408, 1024, 768, 640, 512, 256, 128, 16))
    tn = tn or _pick(N, (1024, 768, 512, 384, 256, 128))
    tk = tk or _pick(K, (512, 256, 128))
    grid = (M // tm, N // tn, K // tk)
    a_spec = pl.BlockSpec((tm, tk), lambda i, j, k: (i, k))
    b_spec = pl.BlockSpec((tk, tn), lambda i, j, k: (k, j))
    o_spec = pl.BlockSpec((tm, tn), lambda i, j, k: (i, j))
    acc = pltpu.VMEM((tm, tn), F32)
    if mode == "plain":
        kern, in_specs, args, scratch = _mm_plain_kernel, [a_spec, b_spec], (a, b), [acc]
    elif mode == "bias":
        kern = _mm_bias_kernel
        in_specs = [a_spec, b_spec, pl.BlockSpec((1, tn), lambda i, j, k: (0, j))]
        args, scratch = (a, b, bias), [acc]
    elif mode == "swiglu":
        nj = N // tn
        kern = _mm_swiglu_kernel
        in_specs = [a_spec, b_spec, pl.BlockSpec((tk, tn), lambda i, j, k: (k, j + nj))]
        args, scratch = (a, b, b), [acc, acc]
    else:
        kern = functools.partial(_mm_resid_kernel, coef=coef, n_lat=n_lat, tm=tm)
        in_specs = [a_spec, b_spec, o_spec, pl.BlockSpec((2, tn), lambda i, j, k: (0, j))]
        args, scratch = (a, b, res, gate), [acc]
    return pl.pallas_call(
        kern,
        grid=grid,
        in_specs=in_specs,
        out_specs=o_spec,
        out_shape=jax.ShapeDtypeStruct((M, N), out_dtype),
        scratch_shapes=scratch,
        compiler_params=_cparams(("parallel", "parallel", "arbitrary")),
        name=f"mm_{mode}",
    )(*args)


def _rowwise(fn, toks, consts, outs, tm, *, halo=False, name="rowwise"):
    M = toks[0].shape[0]
    n_t, n_c = len(toks), len(consts)

    def kern(*refs):
        i = pl.program_id(0)
        vals = [r[...] for r in refs[:n_t + (2 if halo else 0) + n_c]]
        res = fn(i, *vals)
        for r, v in zip(refs[n_t + (2 if halo else 0) + n_c:], res):
            r[...] = v.astype(r.dtype)

    in_specs = [pl.BlockSpec((tm, t.shape[1]), lambda i: (i, 0)) for t in toks]
    args = list(toks)
    if halo:
        width = toks[0].shape[1]
        per, last = tm // SUBLANES, M // SUBLANES - 1
        in_specs.append(pl.BlockSpec((SUBLANES, width), lambda i: (jnp.maximum(i * per - 1, 0), 0)))
        in_specs.append(pl.BlockSpec((SUBLANES, width), lambda i: (jnp.minimum((i + 1) * per, last), 0)))
        args += [toks[0], toks[0]]
    for c in consts:
        in_specs.append(pl.BlockSpec(c.shape, lambda i, nd=c.ndim: (0,) * nd))
        args.append(c)
    return pl.pallas_call(
        kern,
        grid=(M // tm,),
        in_specs=in_specs,
        out_specs=[pl.BlockSpec((tm, w), lambda i: (i, 0)) for w, _ in outs],
        out_shape=[jax.ShapeDtypeStruct((M, w), dt) for w, dt in outs],
        compiler_params=_cparams(("parallel",)),
        name=name,
    )(*args)


def _row_ids(i, tm, shape):
    return i * tm + lax.broadcasted_iota(jnp.int32, shape, 0)


def _conv3(i, x, prev8, next8, w, tm, n_lat, n_all):
    rows = _row_ids(i, tm, x.shape)
    local = lax.broadcasted_iota(jnp.int32, x.shape, 0)
    prev_row = jnp.broadcast_to(prev8[SUBLANES - 1:SUBLANES, :], x.shape)
    next_row = jnp.broadcast_to(next8[0:1, :], x.shape)
    xp = jnp.where(local == 0, prev_row, pltpu.roll(x, 1, 0))
    xn = jnp.where(local == tm - 1, next_row, pltpu.roll(x, tm - 1, 0))
    xp = jnp.where((rows == 0) | (rows == n_lat), 0.0, xp)
    xn = jnp.where((rows == n_lat - 1) | (rows == n_all - 1), 0.0, xn)
    return xp * w[0:1, :] + x * w[1:2, :] + xn * w[2:3, :]


def _rms(x, gain, width):
    ms = jnp.sum(x * x, axis=-1, keepdims=True) * (1.0 / width)
    return x * lax.rsqrt(ms + NORM_EPS) * gain


def _wkv_kernel(rf, vf, kkf, wf, kf, bf, rb, vb, kkb, wb, kb, bb, eh_ref, dm_ref, hs_ref,
                yf_ref, yb_ref, s_ref, *, T):
    @pl.when(pl.program_id(0) == 0)
    def _():
        s_ref[...] = jnp.zeros_like(s_ref)

    eh = eh_ref[...]
    dm = dm_ref[...]
    hs = hs_ref[...]
    tile = (A_HEAD_DIM, WKV_W)
    dirs = ((rf, vf, kkf, wf, kf, bf, yf_ref), (rb, vb, kkb, wb, kb, bb, yb_ref))
    sub = lax.broadcasted_iota(jnp.int32, (SUBLANES, WKV_YW), 0)
    nq, hpq = N_QUADS, WKV_W // A_HEAD_DIM

    def bc(tile8, j, q):
        return jnp.broadcast_to(tile8[j:j + 1, q * WKV_W:(q + 1) * WKV_W], tile)

    def group(g, carry):
        bases = (pl.multiple_of(g * SUBLANES, SUBLANES), pl.multiple_of(T - (g + 1) * SUBLANES, SUBLANES))
        blk = []
        for d, (r_, v_, kk_, w_, k_, b_, _) in enumerate(dirs):
            rows8 = pl.ds(bases[d], SUBLANES)
            blk.append(dict(r=r_[rows8, :], v=v_[rows8, :].astype(BF16), kk=kk_[rows8, :],
                            w=w_[rows8, :], k=k_[rows8, :], b=b_[rows8, :]))
        ytile = [[jnp.zeros((SUBLANES, WKV_YW), F32) for _ in range(hpq)] for _ in range(2)]
        for j in range(SUBLANES):
            js = (j, SUBLANES - 1 - j)
            both = []
            for d in range(2):
                lhs = [(s_ref[d, q] * bc(blk[d]["kk"], js[d], q)).astype(BF16) for q in range(nq)]
                lhs += [bc(blk[d]["v"], js[d], q) * dm for q in range(nq)]
                both.append(_dot(jnp.concatenate(lhs, axis=0), eh))
            for d in range(2):
                qs = []
                for q in range(nq):
                    sa = both[d][q * A_HEAD_DIM:(q + 1) * A_HEAD_DIM]
                    vcol = both[d][(nq + q) * A_HEAD_DIM:(nq + q + 1) * A_HEAD_DIM]
                    s = (s_ref[d, q] * bc(blk[d]["w"], js[d], q) - sa * bc(blk[d]["b"], js[d], q)
                         + vcol * bc(blk[d]["k"], js[d], q))
                    s_ref[d, q] = s
                    qs.append((s * bc(blk[d]["r"], js[d], q)).astype(BF16))
                qs.append(jnp.zeros(tile, BF16))
                y = lax.dot_general(hs, jnp.concatenate(qs, axis=0), (((1,), (1,)), ((), ())),
                                    preferred_element_type=F32)
                for h in range(hpq):
                    ytile[d][h] = jnp.where(sub == js[d], jnp.broadcast_to(y[h:h + 1, :], sub.shape),
                                            ytile[d][h])
        for d in range(2):
            for h in range(hpq):
                dirs[d][6][pl.ds(bases[d], SUBLANES), h * WKV_YW:(h + 1) * WKV_YW] = ytile[d][h]
        return carry

    lax.fori_loop(0, T // SUBLANES, group, 0)


def _wkv(r, v, kk, w0, k0, b0, w1, k1, b1, eh, dm, hs, n_lat):
    M = r.shape[0]
    T = WKV_T
    nx, nb = n_lat // T, M // T
    nc = nb - nx

    def fwd(i):
        return (jnp.where(i < nc, nx + i, i - nc), 0)

    def bwd(i):
        return (jnp.where(i < nc, nx + (nc - 1 - i), nx - 1 - (i - nc)), 0)

    fspec = pl.BlockSpec((T, A_WIDTH), fwd)
    bspec = pl.BlockSpec((T, A_WIDTH), bwd)
    hpq = WKV_W // A_HEAD_DIM
    yw = hpq * WKV_YW
    ys = pl.pallas_call(
        functools.partial(_wkv_kernel, T=T),
        grid=(nb,),
        in_specs=[fspec] * 6 + [bspec] * 6 + [
            pl.BlockSpec(c.shape, lambda i: (0, 0)) for c in (eh, dm, hs)],
        out_specs=[pl.BlockSpec((T, yw), fwd), pl.BlockSpec((T, yw), bwd)],
        out_shape=[jax.ShapeDtypeStruct((M, yw), F32)] * 2,
        scratch_shapes=[pltpu.VMEM((2, N_QUADS, A_HEAD_DIM, WKV_W), F32)],
        compiler_params=_cparams(("arbitrary",)),
        name="wkv7_scan",
    )(r, v, kk, w0, k0, b0, r, v, kk, w1, k1, b1, eh, dm, hs)
    nqp = WKV_YW // A_HEAD_DIM
    return [y.reshape(M, hpq, nqp, A_HEAD_DIM)[:, :, :N_QUADS].transpose(0, 2, 1, 3).reshape(M, A_WIDTH)
            for y in ys]


def _dft_rows_kernel(fc_ref, fs_ref, x_ref, ar_ref, ai_ref):
    x = x_ref[...]
    ar_ref[...] = _dot3(fc_ref[...], x)
    ai_ref[...] = -_dot3(fs_ref[...], x)


def _dft_rows(fc, fs, x2d):
    W = x2d.shape[1]
    n1, n1h = fc.shape
    tn = _pick(W, (4096, 2048, 1024, 512, 256, 128))
    fspec = pl.BlockSpec(fc.shape, lambda j: (0, 0))
    return pl.pallas_call(
        _dft_rows_kernel,
        grid=(W // tn,),
        in_specs=[fspec, fspec, pl.BlockSpec((n1h, tn), lambda j: (0, j))],
        out_specs=[pl.BlockSpec((n1, tn), lambda j: (0, j))] * 2,
        out_shape=[jax.ShapeDtypeStruct((n1, W), F32)] * 2,
        compiler_params=_cparams(("parallel",)),
        name="fft_stage1",
    )(fc, fs, x2d)


def _tile_lanes(t, width):
    return t if width == LANES else jnp.concatenate([t] * (width // LANES), axis=-1)


def _cdot(m3, xr, xi):
    x = jnp.concatenate([xr, xi], axis=0)
    hi, lo = _split(x)
    out = _dot(m3, jnp.concatenate([hi, hi, lo], axis=0))
    return out[:FFT_N2], out[FFT_N2:]


def _fft_fwd_mid(ar, ai, twr, twi, mf):
    return _cdot(mf, ar * twr - ai * twi, ar * twi + ai * twr)


def _filter_spec_kernel(afr, afi, abr, abi, twr_ref, twi_ref, shr_ref, shi_ref, mf_ref, hr_ref, hi_ref):
    ct = afr.shape[-1]
    twr, twi = _tile_lanes(twr_ref[0], ct), _tile_lanes(twi_ref[0], ct)
    shr, shi = _tile_lanes(shr_ref[0], ct), _tile_lanes(shi_ref[0], ct)
    mf = mf_ref[...]
    fr, fi = _fft_fwd_mid(afr[0], afi[0], twr, twi, mf)
    br, bi = _fft_fwd_mid(abr[0], abi[0], twr, twi, mf)
    hr_ref[0] = fr + br * shr + bi * shi
    hi_ref[0] = fi + br * shi - bi * shr


def _conv_spec_kernel(agr, agi, hr, hi, twr_ref, twi_ref, mf_ref, mi_ref, dr_ref, di_ref):
    ct = agr.shape[-1]
    twr, twi = _tile_lanes(twr_ref[0], ct), _tile_lanes(twi_ref[0], ct)
    xr, xi = _fft_fwd_mid(agr[0], agi[0], twr, twi, mf_ref[...])
    yr = xr * hr[0] - xi * hi[0]
    yi = xr * hi[0] + xi * hr[0]
    cr, ci = _cdot(mi_ref[...], yr, yi)
    dr_ref[0] = cr * twr + ci * twi
    di_ref[0] = ci * twr - cr * twi


def _idft_rows_kernel(gc_ref, gs_ref, dr_ref, di_ref, y_ref, *, scale):
    y_ref[...] = scale * (_dot3(gc_ref[...], dr_ref[...]) - _dot3(gs_ref[...], di_ref[...]))


def _idft_rows(gc, gs, dr2d, di2d, scale):
    n1, W = dr2d.shape
    n1h = gc.shape[0]
    tn = _pick(W, (4096, 2048, 1024, 512, 256, 128))
    gspec = pl.BlockSpec(gc.shape, lambda j: (0, 0))
    dspec = pl.BlockSpec((n1, tn), lambda j: (0, j))
    return pl.pallas_call(
        functools.partial(_idft_rows_kernel, scale=scale),
        grid=(W // tn,),
        in_specs=[gspec, gspec, dspec, dspec],
        out_specs=pl.BlockSpec((n1h, tn), lambda j: (0, j)),
        out_shape=jax.ShapeDtypeStruct((n1h, W), F32),
        compiler_params=_cparams(("parallel",)),
        name="fft_stage1_inv",
    )(gc, gs, dr2d, di2d)


def _fft_tables(n_lat):
    n = 2 * n_lat
    n1 = n // FFT_N2

    def cs(rows, cols, period):
        idx = (jnp.arange(rows, dtype=jnp.int32)[:, None] * jnp.arange(cols, dtype=jnp.int32)[None, :]) % period
        ang = idx.astype(F32) * (2.0 * math.pi / period)
        return jnp.cos(ang), jnp.sin(ang)

    n1h = n1 // 2
    nk = n1h + 1
    nkp = -(-nk // SUBLANES) * SUBLANES
    live = (jnp.arange(nkp) < nk).astype(F32)[:, None]
    f1c, f1s = cs(nkp, n1h, n1)
    f1c, f1s = f1c * live, f1s * live
    f2c, f2s = cs(FFT_N2, FFT_N2, FFT_N2)
    twc, tws = cs(nkp, FFT_N2, n)
    shape = (nkp, FFT_N2, LANES)
    twc = jnp.broadcast_to(twc[:, :, None], shape)
    tws = jnp.broadcast_to(tws[:, :, None], shape)
    kidx = (jnp.arange(nkp, dtype=jnp.int32)[:, None] + n1 * jnp.arange(FFT_N2, dtype=jnp.int32)[None, :]) % n
    kang = kidx.astype(F32) * (2.0 * math.pi / n)
    shr = jnp.broadcast_to(jnp.cos(kang)[:, :, None], shape)
    shi = jnp.broadcast_to(jnp.sin(kang)[:, :, None], shape)
    gc, gs = cs(n1h, nkp, n1)
    k1 = jnp.arange(nkp)
    ck = jnp.where((k1 == 0) | (k1 == n1h), 1.0, 2.0) * (k1 < nk)
    gc, gs = gc * ck[None, :], gs * ck[None, :]

    def three_pass(m):
        hi = m.astype(BF16)
        lo = (m - hi.astype(F32)).astype(BF16)
        return jnp.concatenate([hi, lo, hi], axis=1)

    mf = three_pass(jnp.block([[f2c, f2s], [-f2s, f2c]]))
    mi = three_pass(jnp.block([[f2c, -f2s], [f2s, f2c]]))
    return dict(n=n, n1=n1, nkp=nkp, f1c=f1c, f1s=f1s, gc=gc, gs=gs, mf=mf, mi=mi,
                twr=twc, twi=-tws, shr=shr, shi=shi)


def _long_conv(g, h, n_lat, tabs):
    M, C = g.shape
    n, nkp = tabs["n"], tabs["nkp"]
    aqr, aqi = _dft_rows(tabs["f1c"], tabs["f1s"], h.reshape(n_lat // FFT_N2, FFT_N2 * 2 * C))
    aqr = aqr.reshape(nkp, FFT_N2, 2 * C)
    aqi = aqi.reshape(nkp, FFT_N2, 2 * C)
    ct = _pick(C, (512, 256, 128))
    nj = C // ct
    blk_f = pl.BlockSpec((1, FFT_N2, ct), lambda k, j: (k, 0, j))
    blk_p = pl.BlockSpec((1, FFT_N2, ct), lambda k, j: (k, 0, j + nj))
    tw = pl.BlockSpec((1, FFT_N2, LANES), lambda k, j: (k, 0, 0))
    fm = pl.BlockSpec(tabs["mf"].shape, lambda k, j: (0, 0))
    hr, hi = pl.pallas_call(
        _filter_spec_kernel,
        grid=(nkp, nj),
        in_specs=[blk_f, blk_f, blk_p, blk_p, tw, tw, tw, tw, fm],
        out_specs=[blk_f, blk_f],
        out_shape=[jax.ShapeDtypeStruct((nkp, FFT_N2, C), F32)] * 2,
        compiler_params=_cparams(("parallel", "parallel")),
        name="fft_filter_spectrum",
    )(aqr, aqi, aqr, aqi, tabs["twr"], tabs["twi"], tabs["shr"], tabs["shi"], tabs["mf"])
    agr, agi = _dft_rows(tabs["f1c"], tabs["f1s"], g.reshape(M // FFT_N2, FFT_N2 * C))
    agr = agr.reshape(nkp, FFT_N2, C)
    agi = agi.reshape(nkp, FFT_N2, C)
    dr, di = pl.pallas_call(
        _conv_spec_kernel,
        grid=(nkp, nj),
        in_specs=[blk_f] * 4 + [tw, tw, fm, fm],
        out_specs=[blk_f, blk_f],
        out_shape=[jax.ShapeDtypeStruct((nkp, FFT_N2, C), F32)] * 2,
        compiler_params=_cparams(("parallel", "parallel")),
        name="fft_conv_spectrum",
    )(agr, agi, hr, hi, tabs["twr"], tabs["twi"], tabs["mf"], tabs["mi"])
    y2d = _idft_rows(tabs["gc"], tabs["gs"], dr.reshape(nkp, FFT_N2 * C), di.reshape(nkp, FFT_N2 * C),
                     1.0 / n)
    return y2d.reshape(n_lat, C)


def _ctx_conv_kernel(g_ref, hf_ref, hb_ref, y_ref, *, n):
    g = g_ref[...]
    rows = lax.broadcasted_iota(jnp.int32, g.shape, 0)
    y_ref[...] = jnp.zeros_like(y_ref)

    def causal(d, carry):
        y_ref[...] += jnp.where(rows >= d, pltpu.roll(g, d, 0), 0.0) * hf_ref[pl.ds(d, 1), :]
        return carry

    def anticausal(d, carry):
        y_ref[...] += jnp.where(rows < n - d, pltpu.roll(g, n - d, 0), 0.0) * hb_ref[pl.ds(d - 1, 1), :]
        return carry

    lax.fori_loop(0, n, causal, 0)
    lax.fori_loop(1, n, anticausal, 0)


def _ctx_conv(g, h, n_lat, n_ctx):
    C = g.shape[1]
    ct = _pick(C, (256, 128))
    nj = C // ct
    rb = n_lat // n_ctx
    return pl.pallas_call(
        functools.partial(_ctx_conv_kernel, n=n_ctx),
        grid=(nj,),
        in_specs=[pl.BlockSpec((n_ctx, ct), lambda j: (rb, j)),
                  pl.BlockSpec((n_ctx, ct), lambda j: (0, j)),
                  pl.BlockSpec((n_ctx, ct), lambda j: (0, j + nj))],
        out_specs=pl.BlockSpec((n_ctx, ct), lambda j: (0, j)),
        out_shape=jax.ShapeDtypeStruct((n_ctx, C), F32),
        compiler_params=_cparams(("parallel",)),
        name="hyena_ctx_conv",
    )(g, h, h)


def _filter_feats(length):
    t = jnp.linspace(0.0, 1.0, length, dtype=F32)[:, None]
    w = (2.0 * math.pi / length) * jnp.arange(length, dtype=F32)[:, None]
    f = jnp.linspace(1e-4, FILTER_BANDS - 1, FILTER_BANDS, dtype=F32)[None, :]
    z = jnp.concatenate([t, jnp.cos(f * w), -jnp.sin(f * w)], axis=-1)
    return jnp.pad(z, ((0, 0), (0, LANES - FILTER_EMB)))


def _filter_kernel(z_ref, w1, b1, w2, b2, w3, b3, w4, fr, dl, h_ref, s_ref, *, tm, length):
    i = pl.program_id(0)
    z = z_ref[...]
    freq = fr[...]
    h = jnp.sin(freq * (_dot3(z, w1[...]) + b1[...]))
    h = jnp.sin(freq * (_dot3(h, w2[...]) + b2[...]))
    h = jnp.sin(freq * (_dot3(h, w3[...]) + b3[...]))
    h = _dot3(h, w4[...])
    h = h * jnp.exp(-z[:, 0:1] * dl[...])
    rows = _row_ids(i, tm, h.shape)
    cols = lax.broadcasted_iota(jnp.int32, h.shape, 1)
    keep = (cols < B_WIDTH) | (rows < length - 1)
    h = jnp.where(keep, h, 0.0)
    h_ref[...] = h
    part = jnp.sum(jnp.abs(h), axis=0, keepdims=True)

    @pl.when(i == 0)
    def _():
        s_ref[...] = jnp.zeros_like(s_ref)

    s_ref[...] += jnp.broadcast_to(part, s_ref.shape)


def _filter_taps(length, fp):
    z = _filter_feats(length)
    tm = _pick(length, (256, 128))
    consts = [fp["w1"], fp["b1"], fp["w2"], fp["b2"], fp["w3"], fp["b3"], fp["w4"], fp["freq"], fp["deltas"]]
    in_specs = [pl.BlockSpec((tm, LANES), lambda i: (i, 0))]
    in_specs += [pl.BlockSpec(c.shape, lambda i: (0, 0)) for c in consts]
    h, s = pl.pallas_call(
        functools.partial(_filter_kernel, tm=tm, length=length),
        grid=(length // tm,),
        in_specs=in_specs,
        out_specs=[pl.BlockSpec((tm, 2 * B_WIDTH), lambda i: (i, 0)),
                   pl.BlockSpec((SUBLANES, 2 * B_WIDTH), lambda i: (0, 0))],
        out_shape=[jax.ShapeDtypeStruct((length, 2 * B_WIDTH), F32),
                   jax.ShapeDtypeStruct((SUBLANES, 2 * B_WIDTH), F32)],
        compiler_params=_cparams(("arbitrary",)),
        name="hyena_filter",
    )(z, *consts)
    return h, s


def _attn_kernel(q_ref, k_ref, v_ref, o_ref, m_ref, acc_ref):
    j = pl.program_id(2)

    @pl.when(j == 0)
    def _():
        m_ref[...] = jnp.full_like(m_ref, -jnp.inf)
        acc_ref[...] = jnp.zeros_like(acc_ref)

    k, v = k_ref[...], v_ref[...]
    for r0 in range(0, q_ref.shape[0], ATTN_SUB):
        rs = slice(r0, r0 + ATTN_SUB)
        s = lax.dot_general(q_ref[rs, :], k, (((1,), (1,)), ((), ())), preferred_element_type=F32)
        m_old = m_ref[rs, :]
        m_new = jnp.maximum(m_old, jnp.max(s, axis=-1, keepdims=True))
        alpha = jnp.exp(m_old - m_new)
        p = jnp.exp(s - m_new)
        acc_ref[rs, :] = alpha * acc_ref[rs, :] + _dot(p.astype(BF16), v)
        m_ref[rs, :] = m_new

    @pl.when(j == pl.num_programs(2) - 1)
    def _():
        acc = acc_ref[...]
        o_ref[...] = (acc[:, :V_HEAD] / acc[:, V_HEAD:V_HEAD + 1]).astype(o_ref.dtype)


def _attention(q, k, v, q_row0, n_q, k_row0, n_k):
    tq = _pick(n_q, (1024, 512, 256))
    tk = _pick(n_k, (768, 1024, 640, 512, 256))
    assert q_row0 % tq == 0 and k_row0 % tk == 0
    qb, kb = q_row0 // tq, k_row0 // tk
    return pl.pallas_call(
        _attn_kernel,
        grid=(C_HEADS, n_q // tq, n_k // tk),
        in_specs=[pl.BlockSpec((tq, QH), lambda h, i, j: (qb + i, h)),
                  pl.BlockSpec((tk, QH), lambda h, i, j: (kb + j, h)),
                  pl.BlockSpec((tk, QH), lambda h, i, j: (kb + j, h))],
        out_specs=pl.BlockSpec((tq, V_HEAD), lambda h, i, j: (i, h)),
        out_shape=jax.ShapeDtypeStruct((n_q, C_WIDTH), BF16),
        scratch_shapes=[pltpu.VMEM((tq, 1), F32), pltpu.VMEM((tq, QH), F32)],
        compiler_params=_cparams(("parallel", "parallel", "arbitrary")),
        name="mla_attention",
    )(q, k, v)


def _pair_sum(x, eh):
    return jnp.concatenate(
        [_dot2(x[:, p * LANES:(p + 1) * LANES], eh) for p in range(N_PAIRS)], axis=-1)


def _rot_half(t):
    lane = lax.broadcasted_iota(jnp.int32, t.shape, 1)
    up = pltpu.roll(t, LANES - 16, 1)
    down = pltpu.roll(t, 16, 1)
    return jnp.where(lane % 32 < 16, -up, down)


def _layer(xa, mod, p, const, n_lat, need_ctx):
    M = xa.shape[0]
    eh = const["eh"]

    def modrow(k):
        return mod[:, k * D_MODEL:(k + 1) * D_MODEL]

    def norm_mod(x, gain, shift, scale):
        tm = 256

        def fn(i, xb, g, sh, sc):
            rows = _row_ids(i, tm, xb.shape)
            lat = rows < n_lat
            y = _rms(xb, g, D_MODEL)
            return (y * (1.0 + jnp.where(lat, sc[0:1], sc[1:2])) + jnp.where(lat, sh[0:1], sh[1:2]),)

        return _rowwise(fn, [x], [gain, shift, scale], [(D_MODEL, BF16)], tm, name="norm_modulate")[0]

    def ffn(x, gain, w_gu, w_down, k0):
        h = norm_mod(x, gain, modrow(k0), modrow(k0 + 1))
        act = _mm(h, w_gu, mode="swiglu", out_dtype=BF16, tn=1024)
        return _mm(act, w_down, mode="resid", res=x, gate=modrow(k0 + 2), coef=0.5, n_lat=n_lat)

    xa = ffn(xa, p["norm_ffn1"], p["ffn1_gu"], p["ffn1_down"], 0)

    h = norm_mod(xa, p["norm_mix"], modrow(3), modrow(4))
    pa = _mm(h, p["w_in_a"], tn=1408)
    pb = _mm(h, p["w_in_b"], tn=1280)
    pc = _mm(h, p["w_in_c"], tn=C_PAD)

    tma = 128

    def rwkv_prep(i, x, prev8, next8, cw, k_k, k_a, w0, wup0, wup1, a0, aup0, aup1, gup, eh_):
        u = _conv3(i, x, prev8, next8, cw, tma, n_lat, M)
        r, k, v = u[:, :A_WIDTH], u[:, A_WIDTH:2 * A_WIDTH], u[:, 2 * A_WIDTH:3 * A_WIDTH]
        xwa = u[:, 3 * A_WIDTH:3 * A_WIDTH + LANES]
        xg = u[:, 3 * A_WIDTH + LANES:3 * A_WIDTH + 3 * LANES]
        kk = k * k_k
        kk = kk / jnp.maximum(jnp.sqrt(_pair_sum(kk * kk, eh_)), 1e-12)
        g = _dot3(_sigmoid(xg), gup)
        th = jnp.tanh(xwa)
        outs = [r, v, kk, g]
        for wup, aup, d in ((wup0, aup0, 0), (wup1, aup1, 1)):
            logw = -_softplus(-(w0[d:d + 1] + _dot3(th, wup))) - 0.5
            decay = jnp.exp(-jnp.exp(logw))
            a = _sigmoid(a0[d:d + 1] + _dot3(xwa, aup))
            outs += [decay, k * (1.0 + (a - 1.0) * k_a), kk * a]
        return outs

    r, v, kk, g, w0, k0, b0, w1, k1, b1 = _rowwise(
        rwkv_prep, [pa],
        [p["a_conv"], p["a_k_k"], p["a_k_a"], p["a_w0"], p["a_w_up0"], p["a_w_up1"], p["a_a0"],
         p["a_a_up0"], p["a_a_up1"], p["a_g_up"], eh],
        [(A_WIDTH, F32)] * 10, tma, halo=True, name="rwkv_prep")
    yf, yb = _wkv(r, v, kk, w0, k0, b0, w1, k1, b1, const["eh2"], const["dm2"], const["hs"], n_lat)

    def rwkv_out(i, yf_, yb_, r_, v_, g_, k0_, k1_, r_k, ln_w, ln_b, eh_):
        y = yf_ + yb_
        inv = 1.0 / A_HEAD_DIM
        mu = _pair_sum(y, eh_) * inv
        yc = y - mu
        var = _pair_sum(yc * yc, eh_) * inv
        yn = yc * lax.rsqrt(var + GN_EPS) * ln_w + ln_b
        bonus = _pair_sum(r_ * (0.5 * (k0_ + k1_)) * r_k, eh_) * v_
        return ((yn + bonus) * g_,)

    o_a = _rowwise(rwkv_out, [yf, yb, r, v, g, k0, k1], [p["a_r_k"], p["a_ln_w"], p["a_ln_b"], eh],
                   [(A_WIDTH, BF16)], 256, name="rwkv_out")[0]

    tmb = 256

    def hyena_prep(i, x, prev8, next8, cw, cb):
        u = _conv3(i, x, prev8, next8, cw, tmb, n_lat, M) + cb
        return u[:, :B_WIDTH], u[:, B_WIDTH:2 * B_WIDTH] * u[:, 2 * B_WIDTH:]

    x0, gsig = _rowwise(hyena_prep, [pb], [p["b_conv"], p["b_conv_b"]],
                        [(B_WIDTH, F32)] * 2, tmb, halo=True, name="hyena_prep")
    n_ctx = M - n_lat
    hx, sx = _filter_taps(n_lat, p["filt"])
    hc, sc = _filter_taps(n_ctx, p["filt"])
    y_lat = _long_conv(gsig, hx, n_lat, const["fft"])
    y_ctx = _ctx_conv(gsig, hc, n_lat, n_ctx) if need_ctx else jnp.zeros((n_ctx, B_WIDTH), F32)
    yconv = jnp.concatenate([y_lat, y_ctx], axis=0)

    def hyena_out(i, x0_, g_, y_, bias, sx_, sc_):
        rows = _row_ids(i, tmb, x0_.shape)
        nx_ = sx_[0:1, :B_WIDTH] + sx_[0:1, B_WIDTH:]
        nc_ = sc_[0:1, :B_WIDTH] + sc_[0:1, B_WIDTH:]
        norm = jnp.where(rows < n_lat, nx_, nc_)
        return (x0_ * (y_ / norm + g_ * bias),)

    o_b = _rowwise(hyena_out, [x0, gsig, yconv], [p["b_bias"], sx, sc], [(B_WIDTH, BF16)], tmb,
                   name="hyena_out")[0]

    tmc = 256

    def mla_prep(i, x, cos, sin, gq, gkv, gpe):
        cq, ckv = x[:, :Q_RANK], x[:, Q_RANK:Q_RANK + KV_RANK]
        kpe = x[:, Q_RANK + KV_RANK:]
        kpe = _rms(kpe, gpe, QK_ROPE)
        kpe = kpe * cos + _rot_half(kpe) * sin
        return _rms(cq, gq, Q_RANK), _rms(ckv, gkv, KV_RANK), kpe

    cqn, ckvn, kpe = _rowwise(mla_prep, [pc, const["cos"], const["sin"]],
                              [p["c_q_norm"], p["c_kv_norm"], p["c_kn_pe"]],
                              [(Q_RANK, BF16), (KV_RANK, BF16), (LANES, F32)], tmc, name="mla_prep")
    q_raw = _mm(cqn, p["c_q_up"], tn=768)
    kv_raw = _mm(ckvn, p["c_kv_up"], tn=768)
    scale = 1.0 / math.sqrt(QK_DIM)

    def mla_heads(i, qr, kvr, kpe_, cos, sin, gn_q, gpe_q, gn_k):
        qs, ks, vs = [], [], []
        for hh in range(C_HEADS):
            qn = _rms(qr[:, hh * QH:hh * QH + QK_NOPE], gn_q, QK_NOPE)
            qp = _rms(qr[:, hh * QH + QK_NOPE:(hh + 1) * QH], gpe_q, QK_ROPE)
            qp = qp * cos + _rot_half(qp) * sin
            qs += [qn * scale, qp * scale]
            ks += [_rms(kvr[:, hh * QH:hh * QH + QK_NOPE], gn_k, QK_NOPE), kpe_]
            vh = kvr[:, hh * QH + QK_NOPE:(hh + 1) * QH]
            vs += [vh, jnp.ones_like(vh)]
        return jnp.concatenate(qs, axis=-1), jnp.concatenate(ks, axis=-1), jnp.concatenate(vs, axis=-1)

    q, k, vv = _rowwise(mla_heads, [q_raw, kv_raw, kpe, const["cos"], const["sin"]],
                        [p["c_qn_nope"], p["c_qn_pe"], p["c_kn_nope"]],
                        [(C_HEADS * QH, BF16)] * 3, tmc, name="mla_heads")
    oc_x = _attention(q, k, vv, 0, n_lat, 0, M)
    if need_ctx:
        oc_c = _attention(q, k, vv, n_lat, n_ctx, n_lat, n_ctx)
    else:
        oc_c = jnp.zeros((n_ctx, C_WIDTH), BF16)
    o_c = jnp.concatenate([oc_x, oc_c], axis=0)

    o = jnp.concatenate([o_a, o_b, o_c], axis=-1)
    xa = _mm(o, p["w_out"], mode="resid", res=xa, gate=modrow(5), coef=1.0, n_lat=n_lat)
    return ffn(xa, p["norm_ffn2"], p["ffn2_gu"], p["ffn2_down"], 6)


def _pad_cols(w, width):
    return jnp.pad(w, [(0, 0)] * (w.ndim - 1) + [(0, width - w.shape[-1])])


def _pad_rows(w, height):
    return jnp.pad(w, [(0, 0)] * (w.ndim - 2) + [(0, height - w.shape[-2]), (0, 0)])


def _row(v):
    return v.reshape(1, -1)


def kernel(x, c, ctx, c_ctx, ada_down, ada_up, ada_bias, norm_ffn1, norm_mix, norm_ffn2, ffn1_gu, ffn1_down, ffn2_gu, ffn2_down, w_in, w_out, a_conv, a_w0, a_w_up, a_a0, a_a_up, a_g_up, a_k_k, a_k_a, a_r_k, a_ln_w, a_ln_b, b_conv, b_conv_b, b_fw1, b_fb1, b_fw2, b_fb2, b_fw3, b_fb3, b_fw4, b_freq, b_bias, c_q_norm, c_q_up, c_kv_norm, c_kv_up, c_qn_nope, c_qn_pe, c_kn_nope, c_kn_pe):
    n_lat, n_ctx = x.shape[1], ctx.shape[1]
    depth = w_in.shape[0]
    xa = jnp.concatenate([x[0], ctx[0]], axis=0)

    lane = jnp.arange(LANES)
    eh = (lane[:, None] // A_HEAD_DIM == lane[None, :] // A_HEAD_DIM).astype(BF16)
    lane2 = jnp.arange(WKV_W)
    eh2 = (lane2[:, None] // A_HEAD_DIM == lane2[None, :] // A_HEAD_DIM).astype(BF16)
    dm2 = (lane2[None, :] % A_HEAD_DIM == jnp.arange(A_HEAD_DIM)[:, None]).astype(BF16)
    rows_g = n_lat // GRID_W
    row = jnp.broadcast_to(jnp.arange(rows_g, dtype=F32)[:, None], (rows_g, GRID_W)).reshape(-1)
    col = jnp.broadcast_to(jnp.arange(GRID_W, dtype=F32)[None, :], (rows_g, GRID_W)).reshape(-1)
    half = QK_ROPE // 2
    inv = ROPE_THETA ** (-jnp.arange(0, half, 2, dtype=F32) / half)
    ang = jnp.concatenate([row[:, None] * inv, row[:, None] * inv, col[:, None] * inv, col[:, None] * inv], axis=-1)
    cos = jnp.concatenate([jnp.cos(ang), jnp.ones((n_ctx, QK_ROPE), F32)], axis=0)
    sin = jnp.concatenate([jnp.sin(ang), jnp.zeros((n_ctx, QK_ROPE), F32)], axis=0)
    hs = (lane2[None, :] // A_HEAD_DIM == jnp.arange(16)[:, None]).astype(BF16)
    const = dict(eh=eh, eh2=eh2, dm2=dm2, hs=hs, cos=_pad_cols(cos, LANES), sin=_pad_cols(sin, LANES), fft=_fft_tables(n_lat))
    deltas = jnp.abs(jnp.linspace(MIN_DECAY, MAX_DECAY, B_WIDTH, dtype=F32))

    cc = jnp.pad(jnp.concatenate([c, c_ctx[None, :]], axis=0), ((0, 14), (0, 0)))
    cs = _rowwise(lambda i, t: (_silu(t),), [cc], [], [(D_MODEL, BF16)], 16, name="silu")[0]

    q_up = c_q_up.reshape(depth, Q_RANK, C_HEADS, QK_DIM)
    q_up = jnp.pad(q_up, ((0, 0), (0, 0), (0, 0), (0, QH - QK_DIM))).reshape(depth, Q_RANK, C_HEADS * QH)

    for l in range(depth):
        mid = _mm(cs, ada_down[l].astype(BF16), out_dtype=BF16, tn=256)
        mod = _mm(mid, ada_up[l].astype(BF16), mode="bias", bias=_row(ada_bias[l]), tn=1024, tk=256)[:2]
        w_in_l = w_in[l]
        filt = dict(
            w1=_pad_cols(_pad_rows(b_fw1[l], LANES), LANES), b1=_pad_cols(_row(b_fb1[l]), LANES),
            w2=_pad_cols(_pad_rows(b_fw2[l], LANES), LANES), b2=_pad_cols(_row(b_fb2[l]), LANES),
            w3=_pad_cols(_pad_rows(b_fw3[l], LANES), LANES), b3=_pad_cols(_row(b_fb3[l]), LANES),
            w4=_pad_rows(b_fw4[l], LANES), freq=_pad_cols(_row(b_freq[l]), LANES),
            deltas=_row(jnp.tile(deltas, 2)))
        zpad = jnp.zeros((A_DECAY_RANK, A_WIDTH), F32)
        p = dict(
            norm_ffn1=_row(norm_ffn1[l]), norm_mix=_row(norm_mix[l]), norm_ffn2=_row(norm_ffn2[l]),
            ffn1_gu=ffn1_gu[l].astype(BF16), ffn1_down=ffn1_down[l].astype(BF16),
            ffn2_gu=ffn2_gu[l].astype(BF16), ffn2_down=ffn2_down[l].astype(BF16),
            w_in_a=_pad_cols(w_in_l[:, :A_IN], A_PAD).astype(BF16),
            w_in_b=w_in_l[:, A_IN:A_IN + B_IN].astype(BF16),
            w_in_c=_pad_cols(w_in_l[:, A_IN + B_IN:], C_PAD).astype(BF16),
            w_out=w_out[l].astype(BF16),
            a_conv=_pad_cols(a_conv[l], A_PAD), a_k_k=_row(a_k_k[l]), a_k_a=_row(a_k_a[l]),
            a_w0=a_w0[l], a_a0=a_a0[l],
            a_w_up0=jnp.concatenate([a_w_up[l, 0], zpad], axis=0),
            a_w_up1=jnp.concatenate([a_w_up[l, 1], zpad], axis=0),
            a_a_up0=jnp.concatenate([zpad, a_a_up[l, 0]], axis=0),
            a_a_up1=jnp.concatenate([zpad, a_a_up[l, 1]], axis=0),
            a_g_up=_pad_rows(a_g_up[l], 2 * LANES),
            a_r_k=_row(a_r_k[l]), a_ln_w=_row(a_ln_w[l]), a_ln_b=_row(a_ln_b[l]),
            b_conv=b_conv[l], b_conv_b=_row(b_conv_b[l]), b_bias=_row(b_bias[l]), filt=filt,
            c_q_norm=_row(c_q_norm[l]), c_kv_norm=_row(c_kv_norm[l]),
            c_kn_pe=_pad_cols(_row(c_kn_pe[l]), LANES),
            c_q_up=q_up[l].astype(BF16), c_kv_up=c_kv_up[l].astype(BF16),
            c_qn_nope=_row(c_qn_nope[l]), c_qn_pe=_pad_cols(_row(c_qn_pe[l]), LANES),
            c_kn_nope=_row(c_kn_nope[l]),
        )
        xa = _layer(xa, mod, p, const, n_lat, l < depth - 1)
    return xa[:n_lat][None]
```

```python
import functools
import math

import jax
import jax.numpy as jnp
from jax import lax
from jax.experimental import pallas as pl
from jax.experimental.pallas import tpu as pltpu

F32 = jnp.float32
BF16 = jnp.bfloat16

D_MODEL = 4096
GRID_W = 64
A_HEADS = 20
A_HEAD_DIM = 64
A_WIDTH = A_HEADS * A_HEAD_DIM
A_DECAY_RANK = 64
A_ICLR_RANK = 64
A_GATE_RANK = 192
A_IN = 3 * A_WIDTH + A_DECAY_RANK + A_ICLR_RANK + A_GATE_RANK
GN_EPS = 64e-5
B_WIDTH = 1280
B_IN = 3 * B_WIDTH
FILTER_EMB = 33
FILTER_BANDS = (FILTER_EMB - 1) // 2
FILTER_HIDDEN = 64
MIN_DECAY = math.log(1e-2) / 1.5
MAX_DECAY = math.log(1e-2) / 0.3
C_HEADS = 12
QK_NOPE = 128
QK_ROPE = 64
QK_DIM = QK_NOPE + QK_ROPE
V_HEAD = 128
C_WIDTH = C_HEADS * V_HEAD
Q_RANK = 1024
KV_RANK = 512
C_IN = Q_RANK + KV_RANK + QK_ROPE
ROPE_THETA = 10000.0
FFN_HIDDEN = 6144
N_MOD = 9
NORM_EPS = 1e-6

LANES = 128
SUBLANES = 8
VMEM_LIMIT = 56 * 1024 * 1024

A_PAD = 4224
C_PAD = 1664
QH = 256
N_PAIRS = A_HEADS // 2
WKV_W = 256
N_QUADS = A_WIDTH // WKV_W
WKV_T = 128
FFT_N2 = 128
ATTN_SUB = 256


def _cparams(sem):
    return pltpu.CompilerParams(dimension_semantics=sem, vmem_limit_bytes=VMEM_LIMIT)


def _pick(n, cands):
    for c in cands:
        if n % c == 0:
            return c
    raise ValueError(f"no tile for {n}")


def _dot(a, b):
    return jnp.dot(a, b, preferred_element_type=F32)


def _split(a):
    hi = a.astype(BF16)
    lo = (a - hi.astype(F32)).astype(BF16)
    return hi, lo


def _dot3(a, b):
    ah, al = _split(a)
    bh, bl = _split(b)
    return _dot(ah, bh) + _dot(al, bh) + _dot(ah, bl)


def _dot2(a, b_bf16):
    ah, al = _split(a)
    return _dot(ah, b_bf16) + _dot(al, b_bf16)


def _sigmoid(x):
    return 1.0 / (1.0 + jnp.exp(-x))


def _silu(x):
    return x * _sigmoid(x)


def _softplus(x):
    return jnp.maximum(x, 0.0) + jnp.log(1.0 + jnp.exp(-jnp.abs(x)))


def _mm_plain_kernel(a_ref, b_ref, o_ref, acc_ref):
    k = pl.program_id(2)

    @pl.when(k == 0)
    def _():
        acc_ref[...] = jnp.zeros_like(acc_ref)

    acc_ref[...] += _dot(a_ref[...], b_ref[...])

    @pl.when(k == pl.num_programs(2) - 1)
    def _():
        o_ref[...] = acc_ref[...].astype(o_ref.dtype)


def _mm_bias_kernel(a_ref, b_ref, bias_ref, o_ref, acc_ref):
    k = pl.program_id(2)

    @pl.when(k == 0)
    def _():
        acc_ref[...] = jnp.zeros_like(acc_ref)

    acc_ref[...] += _dot(a_ref[...], b_ref[...])

    @pl.when(k == pl.num_programs(2) - 1)
    def _():
        o_ref[...] = (acc_ref[...] + bias_ref[...]).astype(o_ref.dtype)


def _mm_swiglu_kernel(a_ref, bg_ref, bu_ref, o_ref, accg_ref, accu_ref):
    k = pl.program_id(2)

    @pl.when(k == 0)
    def _():
        accg_ref[...] = jnp.zeros_like(accg_ref)
        accu_ref[...] = jnp.zeros_like(accu_ref)

    a = a_ref[...]
    accg_ref[...] += _dot(a, bg_ref[...])
    accu_ref[...] += _dot(a, bu_ref[...])

    @pl.when(k == pl.num_programs(2) - 1)
    def _():
        o_ref[...] = (_silu(accg_ref[...]) * accu_ref[...]).astype(o_ref.dtype)


def _mm_resid_kernel(a_ref, b_ref, res_ref, gate_ref, o_ref, acc_ref, *, coef, n_lat, tm):
    k = pl.program_id(2)

    @pl.when(k == 0)
    def _():
        acc_ref[...] = jnp.zeros_like(acc_ref)

    acc_ref[...] += _dot(a_ref[...], b_ref[...])

    @pl.when(k == pl.num_programs(2) - 1)
    def _():
        rows = pl.program_id(0) * tm + lax.broadcasted_iota(jnp.int32, acc_ref.shape, 0)
        gate = jnp.where(rows < n_lat, gate_ref[0:1, :], gate_ref[1:2, :])
        o_ref[...] = res_ref[...] + coef * gate * acc_ref[...]


def _mm(a, b, *, mode="plain", out_dtype=F32, tn=None, tk=None, bias=None, res=None, gate=None,
        coef=1.0, n_lat=0):
    M, K = a.shape
    N = b.shape[1] // 2 if mode == "swiglu" else b.shape[1]
    tm = _pick(M, (1408, 1024, 768, 640, 512, 256, 128, 16))
    tn = tn or _pick(N, (1024, 768, 512, 384, 256, 128))
    tk = tk or _pick(K, (512, 256, 128))
    grid = (M // tm, N // tn, K // tk)
    a_spec = pl.BlockSpec((tm, tk), lambda i, j, k: (i, k))
    b_spec = pl.BlockSpec((tk, tn), lambda i, j, k: (k, j))
    o_spec = pl.BlockSpec((tm, tn), lambda i, j, k: (i, j))
    acc = pltpu.VMEM((tm, tn), F32)
    if mode == "plain":
        kern, in_specs, args, scratch = _mm_plain_kernel, [a_spec, b_spec], (a, b), [acc]
    elif mode == "bias":
        kern = _mm_bias_kernel
        in_specs = [a_spec, b_spec, pl.BlockSpec((1, tn), lambda i, j, k: (0, j))]
        args, scratch = (a, b, bias), [acc]
    elif mode == "swiglu":
        nj = N // tn
        kern = _mm_swiglu_kernel
        in_specs = [a_spec, b_spec, pl.BlockSpec((tk, tn), lambda i, j, k: (k, j + nj))]
        args, scratch = (a, b, b), [acc, acc]
    else:
        kern = functools.partial(_mm_resid_kernel, coef=coef, n_lat=n_lat, tm=tm)
        in_specs = [a_spec, b_spec, o_spec, pl.BlockSpec((2, tn), lambda i, j, k: (0, j))]
        args, scratch = (a, b, res, gate), [acc]
    return pl.pallas_call(
        kern,
        grid=grid,
        in_specs=in_specs,
        out_specs=o_spec,
        out_shape=jax.ShapeDtypeStruct((M, N), out_dtype),
        scratch_shapes=scratch,
        compiler_params=_cparams(("parallel", "parallel", "arbitrary")),
        name=f"mm_{mode}",
    )(*args)


def _rowwise(fn, toks, consts, outs, tm, *, halo=False, name="rowwise"):
    M = toks[0].shape[0]
    n_t, n_c = len(toks), len(consts)

    def kern(*refs):
        i = pl.program_id(0)
        vals = [r[...] for r in refs[:n_t + (2 if halo else 0) + n_c]]
        res = fn(i, *vals)
        for r, v in zip(refs[n_t + (2 if halo else 0) + n_c:], res):
            r[...] = v.astype(r.dtype)

    in_specs = [pl.BlockSpec((tm, t.shape[1]), lambda i: (i, 0)) for t in toks]
    args = list(toks)
    if halo:
        width = toks[0].shape[1]
        per, last = tm // SUBLANES, M // SUBLANES - 1
        in_specs.append(pl.BlockSpec((SUBLANES, width), lambda i: (jnp.maximum(i * per - 1, 0), 0)))
        in_specs.append(pl.BlockSpec((SUBLANES, width), lambda i: (jnp.minimum((i + 1) * per, last), 0)))
        args += [toks[0], toks[0]]
    for c in consts:
        in_specs.append(pl.BlockSpec(c.shape, lambda i, nd=c.ndim: (0,) * nd))
        args.append(c)
    return pl.pallas_call(
        kern,
        grid=(M // tm,),
        in_specs=in_specs,
        out_specs=[pl.BlockSpec((tm, w), lambda i: (i, 0)) for w, _ in outs],
        out_shape=[jax.ShapeDtypeStruct((M, w), dt) for w, dt in outs],
        compiler_params=_cparams(("parallel",)),
        name=name,
    )(*args)


def _row_ids(i, tm, shape):
    return i * tm + lax.broadcasted_iota(jnp.int32, shape, 0)


def _conv3(i, x, prev8, next8, w, tm, n_lat, n_all):
    rows = _row_ids(i, tm, x.shape)
    local = lax.broadcasted_iota(jnp.int32, x.shape, 0)
    prev_row = jnp.broadcast_to(prev8[SUBLANES - 1:SUBLANES, :], x.shape)
    next_row = jnp.broadcast_to(next8[0:1, :], x.shape)
    xp = jnp.where(local == 0, prev_row, pltpu.roll(x, 1, 0))
    xn = jnp.where(local == tm - 1, next_row, pltpu.roll(x, tm - 1, 0))
    xp = jnp.where((rows == 0) | (rows == n_lat), 0.0, xp)
    xn = jnp.where((rows == n_lat - 1) | (rows == n_all - 1), 0.0, xn)
    return xp * w[0:1, :] + x * w[1:2, :] + xn * w[2:3, :]


def _rms(x, gain, width):
    ms = jnp.sum(x * x, axis=-1, keepdims=True) * (1.0 / width)
    return x * lax.rsqrt(ms + NORM_EPS) * gain


def _wkv_kernel(rf, vf, kkf, wf, kf, bf, rb, vb, kkb, wb, kb, bb, eh_ref, dm_ref,
                yf_ref, yb_ref, s_ref, *, T):
    @pl.when(pl.program_id(0) == 0)
    def _():
        s_ref[...] = jnp.zeros_like(s_ref)

    eh = eh_ref[...]
    dm = dm_ref[...]
    tile = (A_HEAD_DIM, WKV_W)
    dirs = ((rf, vf, kkf, wf, kf, bf, yf_ref), (rb, vb, kkb, wb, kb, bb, yb_ref))
    lane_in_head = lax.broadcasted_iota(jnp.int32, tile, 1) % A_HEAD_DIM
    nq, hpq = N_QUADS, WKV_W // A_HEAD_DIM

    def bc(tile8, j, q):
        return jnp.broadcast_to(tile8[j:j + 1, q * WKV_W:(q + 1) * WKV_W], tile)

    def group(g, carry):
        bases = (pl.multiple_of(g * SUBLANES, SUBLANES), pl.multiple_of(T - (g + 1) * SUBLANES, SUBLANES))
        blk = []
        for d, (r_, v_, kk_, w_, k_, b_, _) in enumerate(dirs):
            rows8 = pl.ds(bases[d], SUBLANES)
            blk.append(dict(r=r_[rows8, :], v=v_[rows8, :].astype(BF16), kk=kk_[rows8, :],
                            w=w_[rows8, :], k=k_[rows8, :], b=b_[rows8, :]))
        vk = []
        for d in range(2):
            lhs = [bc(blk[d]["v"], jj, q) * dm for jj in range(SUBLANES) for q in range(nq)]
            vc = _dot(jnp.concatenate(lhs, axis=0), eh)
            vk.append([[vc[(jj * nq + q) * A_HEAD_DIM:(jj * nq + q + 1) * A_HEAD_DIM] * bc(blk[d]["k"], jj, q)
                        for q in range(nq)] for jj in range(SUBLANES)])
        for j in range(SUBLANES):
            js = (j, SUBLANES - 1 - j)
            sas = []
            for d in range(2):
                lhs = [(s_ref[d, q] * bc(blk[d]["kk"], js[d], q)).astype(BF16) for q in range(nq)]
                sas.append(_dot(jnp.concatenate(lhs, axis=0), eh))
            for d in range(2):
                qs = []
                for q in range(nq):
                    sa = sas[d][q * A_HEAD_DIM:(q + 1) * A_HEAD_DIM]
                    s = (s_ref[d, q] * bc(blk[d]["w"], js[d], q) - sa * bc(blk[d]["b"], js[d], q)
                         + vk[d][js[d]][q])
                    s_ref[d, q] = s
                    qs.append((s * bc(blk[d]["r"], js[d], q)).astype(BF16))
                ysum = _dot(jnp.concatenate(qs, axis=0), eh)
                z = ysum[:A_HEAD_DIM]
                for q in range(1, nq):
                    z = jnp.where(lane_in_head == q, ysum[q * A_HEAD_DIM:(q + 1) * A_HEAD_DIM], z)
                zt = jnp.concatenate([z, jnp.zeros_like(z)], axis=0).T
                for h in range(hpq):
                    dirs[d][6][bases[d] + js[d], h] = zt[h * A_HEAD_DIM:h * A_HEAD_DIM + SUBLANES, :]
        return carry

    lax.fori_loop(0, T // SUBLANES, group, 0)


def _wkv(r, v, kk, w0, k0, b0, w1, k1, b1, eh, dm, n_lat):
    M = r.shape[0]
    T = WKV_T
    nx, nb = n_lat // T, M // T
    nc = nb - nx

    def fwd(i):
        return (jnp.where(i < nc, nx + i, i - nc), 0)

    def bwd(i):
        return (jnp.where(i < nc, nx + (nc - 1 - i), nx - 1 - (i - nc)), 0)

    fspec = pl.BlockSpec((T, A_WIDTH), fwd)
    bspec = pl.BlockSpec((T, A_WIDTH), bwd)
    hpq = WKV_W // A_HEAD_DIM
    yshape = (T, hpq, SUBLANES, LANES)
    ys = pl.pallas_call(
        functools.partial(_wkv_kernel, T=T),
        grid=(nb,),
        in_specs=[fspec] * 6 + [bspec] * 6 + [
            pl.BlockSpec(c.shape, lambda i: (0, 0)) for c in (eh, dm)],
        out_specs=[pl.BlockSpec(yshape, lambda i: fwd(i) + (0, 0)),
                   pl.BlockSpec(yshape, lambda i: bwd(i) + (0, 0))],
        out_shape=[jax.ShapeDtypeStruct((M,) + yshape[1:], F32)] * 2,
        scratch_shapes=[pltpu.VMEM((2, N_QUADS, A_HEAD_DIM, WKV_W), F32)],
        compiler_params=_cparams(("arbitrary",)),
        name="wkv7_scan",
    )(r, v, kk, w0, k0, b0, r, v, kk, w1, k1, b1, eh, dm)
    return [y[:, :, :N_QUADS, :A_HEAD_DIM].transpose(0, 2, 1, 3).reshape(M, A_WIDTH) for y in ys]


def _dft_rows_kernel(fc_ref, fs_ref, x_ref, ar_ref, ai_ref):
    x = x_ref[...]
    ar_ref[...] = _dot3(fc_ref[...], x)
    ai_ref[...] = -_dot3(fs_ref[...], x)


def _dft_rows(fc, fs, x2d):
    W = x2d.shape[1]
    n1, n1h = fc.shape
    tn = _pick(W, (4096, 2048, 1024, 512, 256, 128))
    fspec = pl.BlockSpec(fc.shape, lambda j: (0, 0))
    return pl.pallas_call(
        _dft_rows_kernel,
        grid=(W // tn,),
        in_specs=[fspec, fspec, pl.BlockSpec((n1h, tn), lambda j: (0, j))],
        out_specs=[pl.BlockSpec((n1, tn), lambda j: (0, j))] * 2,
        out_shape=[jax.ShapeDtypeStruct((n1, W), F32)] * 2,
        compiler_params=_cparams(("parallel",)),
        name="fft_stage1",
    )(fc, fs, x2d)


def _tile_lanes(t, width):
    return t if width == LANES else jnp.concatenate([t] * (width // LANES), axis=-1)


def _cdot(m3, xr, xi):
    x = jnp.concatenate([xr, xi], axis=0)
    hi, lo = _split(x)
    out = _dot(m3, jnp.concatenate([hi, hi, lo], axis=0))
    return out[:FFT_N2], out[FFT_N2:]


def _fft_fwd_mid(ar, ai, twr, twi, mf):
    return _cdot(mf, ar * twr - ai * twi, ar * twi + ai * twr)


def _filter_spec_kernel(afr, afi, abr, abi, twr_ref, twi_ref, shr_ref, shi_ref, mf_ref, hr_ref, hi_ref):
    ct = afr.shape[-1]
    twr, twi = _tile_lanes(twr_ref[0], ct), _tile_lanes(twi_ref[0], ct)
    shr, shi = _tile_lanes(shr_ref[0], ct), _tile_lanes(shi_ref[0], ct)
    mf = mf_ref[...]
    fr, fi = _fft_fwd_mid(afr[0], afi[0], twr, twi, mf)
    br, bi = _fft_fwd_mid(abr[0], abi[0], twr, twi, mf)
    hr_ref[0] = fr + br * shr + bi * shi
    hi_ref[0] = fi + br * shi - bi * shr


def _conv_spec_kernel(agr, agi, hr, hi, twr_ref, twi_ref, mf_ref, mi_ref, dr_ref, di_ref):
    ct = agr.shape[-1]
    twr, twi = _tile_lanes(twr_ref[0], ct), _tile_lanes(twi_ref[0], ct)
    xr, xi = _fft_fwd_mid(agr[0], agi[0], twr, twi, mf_ref[...])
    yr = xr * hr[0] - xi * hi[0]
    yi = xr * hi[0] + xi * hr[0]
    cr, ci = _cdot(mi_ref[...], yr, yi)
    dr_ref[0] = cr * twr + ci * twi
    di_ref[0] = ci * twr - cr * twi


def _idft_rows_kernel(gc_ref, gs_ref, dr_ref, di_ref, y_ref, *, scale):
    y_ref[...] = scale * (_dot3(gc_ref[...], dr_ref[...]) - _dot3(gs_ref[...], di_ref[...]))


def _idft_rows(gc, gs, dr2d, di2d, scale):
    n1, W = dr2d.shape
    n1h = gc.shape[0]
    tn = _pick(W, (4096, 2048, 1024, 512, 256, 128))
    gspec = pl.BlockSpec(gc.shape, lambda j: (0, 0))
    dspec = pl.BlockSpec((n1, tn), lambda j: (0, j))
    return pl.pallas_call(
        functools.partial(_idft_rows_kernel, scale=scale),
        grid=(W // tn,),
        in_specs=[gspec, gspec, dspec, dspec],
        out_specs=pl.BlockSpec((n1h, tn), lambda j: (0, j)),
        out_shape=jax.ShapeDtypeStruct((n1h, W), F32),
        compiler_params=_cparams(("parallel",)),
        name="fft_stage1_inv",
    )(gc, gs, dr2d, di2d)


def _fft_tables(n_lat):
    n = 2 * n_lat
    n1 = n // FFT_N2

    def cs(rows, cols, period):
        idx = (jnp.arange(rows, dtype=jnp.int32)[:, None] * jnp.arange(cols, dtype=jnp.int32)[None, :]) % period
        ang = idx.astype(F32) * (2.0 * math.pi / period)
        return jnp.cos(ang), jnp.sin(ang)

    n1h = n1 // 2
    nk = n1h + 1
    nkp = -(-nk // SUBLANES) * SUBLANES
    live = (jnp.arange(nkp) < nk).astype(F32)[:, None]
    f1c, f1s = cs(nkp, n1h, n1)
    f1c, f1s = f1c * live, f1s * live
    f2c, f2s = cs(FFT_N2, FFT_N2, FFT_N2)
    twc, tws = cs(nkp, FFT_N2, n)
    shape = (nkp, FFT_N2, LANES)
    twc = jnp.broadcast_to(twc[:, :, None], shape)
    tws = jnp.broadcast_to(tws[:, :, None], shape)
    kidx = (jnp.arange(nkp, dtype=jnp.int32)[:, None] + n1 * jnp.arange(FFT_N2, dtype=jnp.int32)[None, :]) % n
    kang = kidx.astype(F32) * (2.0 * math.pi / n)
    shr = jnp.broadcast_to(jnp.cos(kang)[:, :, None], shape)
    shi = jnp.broadcast_to(jnp.sin(kang)[:, :, None], shape)
    gc, gs = cs(n1h, nkp, n1)
    k1 = jnp.arange(nkp)
    ck = jnp.where((k1 == 0) | (k1 == n1h), 1.0, 2.0) * (k1 < nk)
    gc, gs = gc * ck[None, :], gs * ck[None, :]

    def three_pass(m):
        hi = m.astype(BF16)
        lo = (m - hi.astype(F32)).astype(BF16)
        return jnp.concatenate([hi, lo, hi], axis=1)

    mf = three_pass(jnp.block([[f2c, f2s], [-f2s, f2c]]))
    mi = three_pass(jnp.block([[f2c, -f2s], [f2s, f2c]]))
    return dict(n=n, n1=n1, nkp=nkp, f1c=f1c, f1s=f1s, gc=gc, gs=gs, mf=mf, mi=mi,
                twr=twc, twi=-tws, shr=shr, shi=shi)


def _long_conv(g, h, n_lat, tabs):
    M, C = g.shape
    n, nkp = tabs["n"], tabs["nkp"]
    aqr, aqi = _dft_rows(tabs["f1c"], tabs["f1s"], h.reshape(n_lat // FFT_N2, FFT_N2 * 2 * C))
    aqr = aqr.reshape(nkp, FFT_N2, 2 * C)
    aqi = aqi.reshape(nkp, FFT_N2, 2 * C)
    ct = _pick(C, (512, 256, 128))
    nj = C // ct
    blk_f = pl.BlockSpec((1, FFT_N2, ct), lambda k, j: (k, 0, j))
    blk_p = pl.BlockSpec((1, FFT_N2, ct), lambda k, j: (k, 0, j + nj))
    tw = pl.BlockSpec((1, FFT_N2, LANES), lambda k, j: (k, 0, 0))
    fm = pl.BlockSpec(tabs["mf"].shape, lambda k, j: (0, 0))
    hr, hi = pl.pallas_call(
        _filter_spec_kernel,
        grid=(nkp, nj),
        in_specs=[blk_f, blk_f, blk_p, blk_p, tw, tw, tw, tw, fm],
        out_specs=[blk_f, blk_f],
        out_shape=[jax.ShapeDtypeStruct((nkp, FFT_N2, C), F32)] * 2,
        compiler_params=_cparams(("parallel", "parallel")),
        name="fft_filter_spectrum",
    )(aqr, aqi, aqr, aqi, tabs["twr"], tabs["twi"], tabs["shr"], tabs["shi"], tabs["mf"])
    agr, agi = _dft_rows(tabs["f1c"], tabs["f1s"], g.reshape(M // FFT_N2, FFT_N2 * C))
    agr = agr.reshape(nkp, FFT_N2, C)
    agi = agi.reshape(nkp, FFT_N2, C)
    dr, di = pl.pallas_call(
        _conv_spec_kernel,
        grid=(nkp, nj),
        in_specs=[blk_f] * 4 + [tw, tw, fm, fm],
        out_specs=[blk_f, blk_f],
        out_shape=[jax.ShapeDtypeStruct((nkp, FFT_N2, C), F32)] * 2,
        compiler_params=_cparams(("parallel", "parallel")),
        name="fft_conv_spectrum",
    )(agr, agi, hr, hi, tabs["twr"], tabs["twi"], tabs["mf"], tabs["mi"])
    y2d = _idft_rows(tabs["gc"], tabs["gs"], dr.reshape(nkp, FFT_N2 * C), di.reshape(nkp, FFT_N2 * C),
                     1.0 / n)
    return y2d.reshape(n_lat, C)


def _ctx_conv_kernel(g_ref, hf_ref, hb_ref, y_ref, *, n):
    g = g_ref[...]
    rows = lax.broadcasted_iota(jnp.int32, g.shape, 0)
    y_ref[...] = jnp.zeros_like(y_ref)

    def causal(d, carry):
        y_ref[...] += jnp.where(rows >= d, pltpu.roll(g, d, 0), 0.0) * hf_ref[pl.ds(d, 1), :]
        return carry

    def anticausal(d, carry):
        y_ref[...] += jnp.where(rows < n - d, pltpu.roll(g, n - d, 0), 0.0) * hb_ref[pl.ds(d - 1, 1), :]
        return carry

    lax.fori_loop(0, n, causal, 0)
    lax.fori_loop(1, n, anticausal, 0)


def _ctx_conv(g, h, n_lat, n_ctx):
    C = g.shape[1]
    ct = _pick(C, (256, 128))
    nj = C // ct
    rb = n_lat // n_ctx
    return pl.pallas_call(
        functools.partial(_ctx_conv_kernel, n=n_ctx),
        grid=(nj,),
        in_specs=[pl.BlockSpec((n_ctx, ct), lambda j: (rb, j)),
                  pl.BlockSpec((n_ctx, ct), lambda j: (0, j)),
                  pl.BlockSpec((n_ctx, ct), lambda j: (0, j + nj))],
        out_specs=pl.BlockSpec((n_ctx, ct), lambda j: (0, j)),
        out_shape=jax.ShapeDtypeStruct((n_ctx, C), F32),
        compiler_params=_cparams(("parallel",)),
        name="hyena_ctx_conv",
    )(g, h, h)


def _filter_feats(length):
    t = jnp.linspace(0.0, 1.0, length, dtype=F32)[:, None]
    w = (2.0 * math.pi / length) * jnp.arange(length, dtype=F32)[:, None]
    f = jnp.linspace(1e-4, FILTER_BANDS - 1, FILTER_BANDS, dtype=F32)[None, :]
    z = jnp.concatenate([t, jnp.cos(f * w), -jnp.sin(f * w)], axis=-1)
    return jnp.pad(z, ((0, 0), (0, LANES - FILTER_EMB)))


def _filter_kernel(z_ref, w1, b1, w2, b2, w3, b3, w4, fr, dl, h_ref, s_ref, *, tm, length):
    i = pl.program_id(0)
    z = z_ref[...]
    freq = fr[...]
    h = jnp.sin(freq * (_dot3(z, w1[...]) + b1[...]))
    h = jnp.sin(freq * (_dot3(h, w2[...]) + b2[...]))
    h = jnp.sin(freq * (_dot3(h, w3[...]) + b3[...]))
    h = _dot3(h, w4[...])
    h = h * jnp.exp(-z[:, 0:1] * dl[...])
    rows = _row_ids(i, tm, h.shape)
    cols = lax.broadcasted_iota(jnp.int32, h.shape, 1)
    keep = (cols < B_WIDTH) | (rows < length - 1)
    h = jnp.where(keep, h, 0.0)
    h_ref[...] = h
    part = jnp.sum(jnp.abs(h), axis=0, keepdims=True)

    @pl.when(i == 0)
    def _():
        s_ref[...] = jnp.zeros_like(s_ref)

    s_ref[...] += jnp.broadcast_to(part, s_ref.shape)


def _filter_taps(length, fp):
    z = _filter_feats(length)
    tm = _pick(length, (256, 128))
    consts = [fp["w1"], fp["b1"], fp["w2"], fp["b2"], fp["w3"], fp["b3"], fp["w4"], fp["freq"], fp["deltas"]]
    in_specs = [pl.BlockSpec((tm, LANES), lambda i: (i, 0))]
    in_specs += [pl.BlockSpec(c.shape, lambda i: (0, 0)) for c in consts]
    h, s = pl.pallas_call(
        functools.partial(_filter_kernel, tm=tm, length=length),
        grid=(length // tm,),
        in_specs=in_specs,
        out_specs=[pl.BlockSpec((tm, 2 * B_WIDTH), lambda i: (i, 0)),
                   pl.BlockSpec((SUBLANES, 2 * B_WIDTH), lambda i: (0, 0))],
        out_shape=[jax.ShapeDtypeStruct((length, 2 * B_WIDTH), F32),
                   jax.ShapeDtypeStruct((SUBLANES, 2 * B_WIDTH), F32)],
        compiler_params=_cparams(("arbitrary",)),
        name="hyena_filter",
    )(z, *consts)
    return h, s


def _attn_kernel(q_ref, k_ref, v_ref, o_ref, m_ref, acc_ref, *, tk):
    m_ref[...] = jnp.full_like(m_ref, -jnp.inf)
    acc_ref[...] = jnp.zeros_like(acc_ref)
    for c0 in range(0, k_ref.shape[0], tk):
        k, v = k_ref[c0:c0 + tk, :], v_ref[c0:c0 + tk, :]
        for r0 in range(0, q_ref.shape[0], ATTN_SUB):
            rs = slice(r0, r0 + ATTN_SUB)
            s = lax.dot_general(q_ref[rs, :], k, (((1,), (1,)), ((), ())), preferred_element_type=F32)
            m_old = m_ref[rs, :]
            m_new = jnp.maximum(m_old, jnp.max(s, axis=-1, keepdims=True))
            alpha = jnp.exp(m_old - m_new)
            p = jnp.exp(s - m_new)
            acc_ref[rs, :] = alpha * acc_ref[rs, :] + _dot(p.astype(BF16), v)
            m_ref[rs, :] = m_new
    acc = acc_ref[...]
    o_ref[...] = (acc[:, :V_HEAD] / acc[:, V_HEAD:V_HEAD + 1]).astype(o_ref.dtype)


def _attention(q, k, v, q_row0, n_q, k_row0, n_k):
    tq = _pick(n_q, (1024, 512, 256))
    tk = _pick(n_k, (768, 1024, 640, 512, 256))
    assert q_row0 % tq == 0 and k_row0 % n_k == 0
    qb, kb = q_row0 // tq, k_row0 // n_k
    return pl.pallas_call(
        functools.partial(_attn_kernel, tk=tk),
        grid=(C_HEADS, n_q // tq),
        in_specs=[pl.BlockSpec((tq, QH), lambda h, i: (qb + i, h)),
                  pl.BlockSpec((n_k, QH), lambda h, i: (kb, h)),
                  pl.BlockSpec((n_k, QH), lambda h, i: (kb, h))],
        out_specs=pl.BlockSpec((tq, V_HEAD), lambda h, i: (i, h)),
        out_shape=jax.ShapeDtypeStruct((n_q, C_WIDTH), BF16),
        scratch_shapes=[pltpu.VMEM((tq, 1), F32), pltpu.VMEM((tq, QH), F32)],
        compiler_params=_cparams(("parallel", "parallel")),
        name="mla_attention",
    )(q, k, v)


def _pair_sum(x, eh):
    return jnp.concatenate(
        [_dot2(x[:, p * LANES:(p + 1) * LANES], eh) for p in range(N_PAIRS)], axis=-1)


def _rot_half(t):
    lane = lax.broadcasted_iota(jnp.int32, t.shape, 1)
    up = pltpu.roll(t, LANES - 16, 1)
    down = pltpu.roll(t, 16, 1)
    return jnp.where(lane % 32 < 16, -up, down)


def _layer(xa, mod, p, const, n_lat, need_ctx):
    M = xa.shape[0]
    eh = const["eh"]

    def modrow(k):
        return mod[:, k * D_MODEL:(k + 1) * D_MODEL]

    def norm_mod(x, gain, shift, scale):
        tm = 256

        def fn(i, xb, g, sh, sc):
            rows = _row_ids(i, tm, xb.shape)
            lat = rows < n_lat
            y = _rms(xb, g, D_MODEL)
            return (y * (1.0 + jnp.where(lat, sc[0:1], sc[1:2])) + jnp.where(lat, sh[0:1], sh[1:2]),)

        return _rowwise(fn, [x], [gain, shift, scale], [(D_MODEL, BF16)], tm, name="norm_modulate")[0]

    def ffn(x, gain, w_gu, w_down, k0):
        h = norm_mod(x, gain, modrow(k0), modrow(k0 + 1))
        act = _mm(h, w_gu, mode="swiglu", out_dtype=BF16, tn=1024)
        return _mm(act, w_down, mode="resid", res=x, gate=modrow(k0 + 2), coef=0.5, n_lat=n_lat)

    xa = ffn(xa, p["norm_ffn1"], p["ffn1_gu"], p["ffn1_down"], 0)

    h = norm_mod(xa, p["norm_mix"], modrow(3), modrow(4))
    pa = _mm(h, p["w_in_a"], tn=1408)
    pb = _mm(h, p["w_in_b"], tn=1280)
    pc = _mm(h, p["w_in_c"], tn=C_PAD)

    tma = 128

    def rwkv_prep(i, x, prev8, next8, cw, k_k, k_a, w0, wup0, wup1, a0, aup0, aup1, gup, eh_):
        u = _conv3(i, x, prev8, next8, cw, tma, n_lat, M)
        r, k, v = u[:, :A_WIDTH], u[:, A_WIDTH:2 * A_WIDTH], u[:, 2 * A_WIDTH:3 * A_WIDTH]
        xwa = u[:, 3 * A_WIDTH:3 * A_WIDTH + LANES]
        xg = u[:, 3 * A_WIDTH + LANES:3 * A_WIDTH + 3 * LANES]
        kk = k * k_k
        kk = kk / jnp.maximum(jnp.sqrt(_pair_sum(kk * kk, eh_)), 1e-12)
        g = _dot3(_sigmoid(xg), gup)
        th = jnp.tanh(xwa)
        outs = [r, v, kk, g]
        for wup, aup, d in ((wup0, aup0, 0), (wup1, aup1, 1)):
            logw = -_softplus(-(w0[d:d + 1] + _dot3(th, wup))) - 0.5
            decay = jnp.exp(-jnp.exp(logw))
            a = _sigmoid(a0[d:d + 1] + _dot3(xwa, aup))
            outs += [decay, k * (1.0 + (a - 1.0) * k_a), kk * a]
        return outs

    r, v, kk, g, w0, k0, b0, w1, k1, b1 = _rowwise(
        rwkv_prep, [pa],
        [p["a_conv"], p["a_k_k"], p["a_k_a"], p["a_w0"], p["a_w_up0"], p["a_w_up1"], p["a_a0"],
         p["a_a_up0"], p["a_a_up1"], p["a_g_up"], eh],
        [(A_WIDTH, F32)] * 10, tma, halo=True, name="rwkv_prep")
    yf, yb = _wkv(r, v, kk, w0, k0, b0, w1, k1, b1, const["eh2"], const["dm2"], n_lat)

    def rwkv_out(i, yf_, yb_, r_, v_, g_, k0_, k1_, r_k, ln_w, ln_b, eh_):
        y = yf_ + yb_
        inv = 1.0 / A_HEAD_DIM
        mu = _pair_sum(y, eh_) * inv
        yc = y - mu
        var = _pair_sum(yc * yc, eh_) * inv
        yn = yc * lax.rsqrt(var + GN_EPS) * ln_w + ln_b
        bonus = _pair_sum(r_ * (0.5 * (k0_ + k1_)) * r_k, eh_) * v_
        return ((yn + bonus) * g_,)

    o_a = _rowwise(rwkv_out, [yf, yb, r, v, g, k0, k1], [p["a_r_k"], p["a_ln_w"], p["a_ln_b"], eh],
                   [(A_WIDTH, BF16)], 256, name="rwkv_out")[0]

    tmb = 256

    def hyena_prep(i, x, prev8, next8, cw, cb):
        u = _conv3(i, x, prev8, next8, cw, tmb, n_lat, M) + cb
        return u[:, :B_WIDTH], u[:, B_WIDTH:2 * B_WIDTH] * u[:, 2 * B_WIDTH:]

    x0, gsig = _rowwise(hyena_prep, [pb], [p["b_conv"], p["b_conv_b"]],
                        [(B_WIDTH, F32)] * 2, tmb, halo=True, name="hyena_prep")
    n_ctx = M - n_lat
    hx, sx = _filter_taps(n_lat, p["filt"])
    hc, sc = _filter_taps(n_ctx, p["filt"])
    y_lat = _long_conv(gsig, hx, n_lat, const["fft"])
    y_ctx = _ctx_conv(gsig, hc, n_lat, n_ctx) if need_ctx else jnp.zeros((n_ctx, B_WIDTH), F32)
    yconv = jnp.concatenate([y_lat, y_ctx], axis=0)

    def hyena_out(i, x0_, g_, y_, bias, sx_, sc_):
        rows = _row_ids(i, tmb, x0_.shape)
        nx_ = sx_[0:1, :B_WIDTH] + sx_[0:1, B_WIDTH:]
        nc_ = sc_[0:1, :B_WIDTH] + sc_[0:1, B_WIDTH:]
        norm = jnp.where(rows < n_lat, nx_, nc_)
        return (x0_ * (y_ / norm + g_ * bias),)

    o_b = _rowwise(hyena_out, [x0, gsig, yconv], [p["b_bias"], sx, sc], [(B_WIDTH, BF16)], tmb,
                   name="hyena_out")[0]

    tmc = 256

    def mla_prep(i, x, cos, sin, gq, gkv, gpe):
        cq, ckv = x[:, :Q_RANK], x[:, Q_RANK:Q_RANK + KV_RANK]
        kpe = x[:, Q_RANK + KV_RANK:]
        kpe = _rms(kpe, gpe, QK_ROPE)
        kpe = kpe * cos + _rot_half(kpe) * sin
        return _rms(cq, gq, Q_RANK), _rms(ckv, gkv, KV_RANK), kpe

    cqn, ckvn, kpe = _rowwise(mla_prep, [pc, const["cos"], const["sin"]],
                              [p["c_q_norm"], p["c_kv_norm"], p["c_kn_pe"]],
                              [(Q_RANK, BF16), (KV_RANK, BF16), (LANES, F32)], tmc, name="mla_prep")
    q_raw = _mm(cqn, p["c_q_up"], tn=768)
    kv_raw = _mm(ckvn, p["c_kv_up"], tn=768)
    scale = 1.0 / math.sqrt(QK_DIM)

    def mla_heads(i, qr, kvr, kpe_, cos, sin, gn_q, gpe_q, gn_k):
        qs, ks, vs = [], [], []
        for hh in range(C_HEADS):
            qn = _rms(qr[:, hh * QH:hh * QH + QK_NOPE], gn_q, QK_NOPE)
            qp = _rms(qr[:, hh * QH + QK_NOPE:(hh + 1) * QH], gpe_q, QK_ROPE)
            qp = qp * cos + _rot_half(qp) * sin
            qs += [qn * scale, qp * scale]
            ks += [_rms(kvr[:, hh * QH:hh * QH + QK_NOPE], gn_k, QK_NOPE), kpe_]
            vh = kvr[:, hh * QH + QK_NOPE:(hh + 1) * QH]
            vs += [vh, jnp.ones_like(vh)]
        return jnp.concatenate(qs, axis=-1), jnp.concatenate(ks, axis=-1), jnp.concatenate(vs, axis=-1)

    q, k, vv = _rowwise(mla_heads, [q_raw, kv_raw, kpe, const["cos"], const["sin"]],
                        [p["c_qn_nope"], p["c_qn_pe"], p["c_kn_nope"]],
                        [(C_HEADS * QH, BF16)] * 3, tmc, name="mla_heads")
    oc_x = _attention(q, k, vv, 0, n_lat, 0, M)
    if need_ctx:
        oc_c = _attention(q, k, vv, n_lat, n_ctx, n_lat, n_ctx)
    else:
        oc_c = jnp.zeros((n_ctx, C_WIDTH), BF16)
    o_c = jnp.concatenate([oc_x, oc_c], axis=0)

    o = jnp.concatenate([o_a, o_b, o_c], axis=-1)
    xa = _mm(o, p["w_out"], mode="resid", res=xa, gate=modrow(5), coef=1.0, n_lat=n_lat)
    return ffn(xa, p["norm_ffn2"], p["ffn2_gu"], p["ffn2_down"], 6)


def _pad_cols(w, width):
    return jnp.pad(w, [(0, 0)] * (w.ndim - 1) + [(0, width - w.shape[-1])])


def _pad_rows(w, height):
    return jnp.pad(w, [(0, 0)] * (w.ndim - 2) + [(0, height - w.shape[-2]), (0, 0)])


def _row(v):
    return v.reshape(1, -1)


def kernel(x, c, ctx, c_ctx, ada_down, ada_up, ada_bias, norm_ffn1, norm_mix, norm_ffn2, ffn1_gu, ffn1_down, ffn2_gu, ffn2_down, w_in, w_out, a_conv, a_w0, a_w_up, a_a0, a_a_up, a_g_up, a_k_k, a_k_a, a_r_k, a_ln_w, a_ln_b, b_conv, b_conv_b, b_fw1, b_fb1, b_fw2, b_fb2, b_fw3, b_fb3, b_fw4, b_freq, b_bias, c_q_norm, c_q_up, c_kv_norm, c_kv_up, c_qn_nope, c_qn_pe, c_kn_nope, c_kn_pe):
    n_lat, n_ctx = x.shape[1], ctx.shape[1]
    depth = w_in.shape[0]
    xa = jnp.concatenate([x[0], ctx[0]], axis=0)

    lane = jnp.arange(LANES)
    eh = (lane[:, None] // A_HEAD_DIM == lane[None, :] // A_HEAD_DIM).astype(BF16)
    lane2 = jnp.arange(WKV_W)
    eh2 = (lane2[:, None] // A_HEAD_DIM == lane2[None, :] // A_HEAD_DIM).astype(BF16)
    dm2 = (lane2[None, :] % A_HEAD_DIM == jnp.arange(A_HEAD_DIM)[:, None]).astype(BF16)
    rows_g = n_lat // GRID_W
    row = jnp.broadcast_to(jnp.arange(rows_g, dtype=F32)[:, None], (rows_g, GRID_W)).reshape(-1)
    col = jnp.broadcast_to(jnp.arange(GRID_W, dtype=F32)[None, :], (rows_g, GRID_W)).reshape(-1)
    half = QK_ROPE // 2
    inv = ROPE_THETA ** (-jnp.arange(0, half, 2, dtype=F32) / half)
    ang = jnp.concatenate([row[:, None] * inv, row[:, None] * inv, col[:, None] * inv, col[:, None] * inv], axis=-1)
    cos = jnp.concatenate([jnp.cos(ang), jnp.ones((n_ctx, QK_ROPE), F32)], axis=0)
    sin = jnp.concatenate([jnp.sin(ang), jnp.zeros((n_ctx, QK_ROPE), F32)], axis=0)
    const = dict(eh=eh, eh2=eh2, dm2=dm2, cos=_pad_cols(cos, LANES), sin=_pad_cols(sin, LANES), fft=_fft_tables(n_lat))
    deltas = jnp.abs(jnp.linspace(MIN_DECAY, MAX_DECAY, B_WIDTH, dtype=F32))

    cc = jnp.pad(jnp.concatenate([c, c_ctx[None, :]], axis=0), ((0, 14), (0, 0)))
    cs = _rowwise(lambda i, t: (_silu(t),), [cc], [], [(D_MODEL, BF16)], 16, name="silu")[0]

    q_up = c_q_up.reshape(depth, Q_RANK, C_HEADS, QK_DIM)
    q_up = jnp.pad(q_up, ((0, 0), (0, 0), (0, 0), (0, QH - QK_DIM))).reshape(depth, Q_RANK, C_HEADS * QH)

    for l in range(depth):
        mid = _mm(cs, ada_down[l].astype(BF16), out_dtype=BF16, tn=256)
        mod = _mm(mid, ada_up[l].astype(BF16), mode="bias", bias=_row(ada_bias[l]), tn=1024, tk=256)[:2]
        w_in_l = w_in[l]
        filt = dict(
            w1=_pad_cols(_pad_rows(b_fw1[l], LANES), LANES), b1=_pad_cols(_row(b_fb1[l]), LANES),
            w2=_pad_cols(_pad_rows(b_fw2[l], LANES), LANES), b2=_pad_cols(_row(b_fb2[l]), LANES),
            w3=_pad_cols(_pad_rows(b_fw3[l], LANES), LANES), b3=_pad_cols(_row(b_fb3[l]), LANES),
            w4=_pad_rows(b_fw4[l], LANES), freq=_pad_cols(_row(b_freq[l]), LANES),
            deltas=_row(jnp.tile(deltas, 2)))
        zpad = jnp.zeros((A_DECAY_RANK, A_WIDTH), F32)
        p = dict(
            norm_ffn1=_row(norm_ffn1[l]), norm_mix=_row(norm_mix[l]), norm_ffn2=_row(norm_ffn2[l]),
            ffn1_gu=ffn1_gu[l].astype(BF16), ffn1_down=ffn1_down[l].astype(BF16),
            ffn2_gu=ffn2_gu[l].astype(BF16), ffn2_down=ffn2_down[l].astype(BF16),
            w_in_a=_pad_cols(w_in_l[:, :A_IN], A_PAD).astype(BF16),
            w_in_b=w_in_l[:, A_IN:A_IN + B_IN].astype(BF16),
            w_in_c=_pad_cols(w_in_l[:, A_IN + B_IN:], C_PAD).astype(BF16),
            w_out=w_out[l].astype(BF16),
            a_conv=_pad_cols(a_conv[l], A_PAD), a_k_k=_row(a_k_k[l]), a_k_a=_row(a_k_a[l]),
            a_w0=a_w0[l], a_a0=a_a0[l],
            a_w_up0=jnp.concatenate([a_w_up[l, 0], zpad], axis=0),
            a_w_up1=jnp.concatenate([a_w_up[l, 1], zpad], axis=0),
            a_a_up0=jnp.concatenate([zpad, a_a_up[l, 0]], axis=0),
            a_a_up1=jnp.concatenate([zpad, a_a_up[l, 1]], axis=0),
            a_g_up=_pad_rows(a_g_up[l], 2 * LANES),
            a_r_k=_row(a_r_k[l]), a_ln_w=_row(a_ln_w[l]), a_ln_b=_row(a_ln_b[l]),
            b_conv=b_conv[l], b_conv_b=_row(b_conv_b[l]), b_bias=_row(b_bias[l]), filt=filt,
            c_q_norm=_row(c_q_norm[l]), c_kv_norm=_row(c_kv_norm[l]),
            c_kn_pe=_pad_cols(_row(c_kn_pe[l]), LANES),
            c_q_up=q_up[l].astype(BF16), c_kv_up=c_kv_up[l].astype(BF16),
            c_qn_nope=_row(c_qn_nope[l]), c_qn_pe=_pad_cols(_row(c_qn_pe[l]), LANES),
            c_kn_nope=_row(c_kn_nope[l]),
        )
        xa = _layer(xa, mod, p, const, n_lat, l < depth - 1)
    return xa[:n_lat][None]
```

```python
import functools
import math

import jax
import jax.numpy as jnp
from jax import lax
from jax.experimental import pallas as pl
from jax.experimental.pallas import tpu as pltpu

F32 = jnp.float32
BF16 = jnp.bfloat16

D_MODEL = 4096
GRID_W = 64
A_HEADS = 20
A_HEAD_DIM = 64
A_WIDTH = A_HEADS * A_HEAD_DIM
A_DECAY_RANK = 64
A_ICLR_RANK = 64
A_GATE_RANK = 192
A_IN = 3 * A_WIDTH + A_DECAY_RANK + A_ICLR_RANK + A_GATE_RANK
GN_EPS = 64e-5
B_WIDTH = 1280
B_IN = 3 * B_WIDTH
FILTER_EMB = 33
FILTER_BANDS = (FILTER_EMB - 1) // 2
FILTER_HIDDEN = 64
MIN_DECAY = math.log(1e-2) / 1.5
MAX_DECAY = math.log(1e-2) / 0.3
C_HEADS = 12
QK_NOPE = 128
QK_ROPE = 64
QK_DIM = QK_NOPE + QK_ROPE
V_HEAD = 128
C_WIDTH = C_HEADS * V_HEAD
Q_RANK = 1024
KV_RANK = 512
C_IN = Q_RANK + KV_RANK + QK_ROPE
ROPE_THETA = 10000.0
FFN_HIDDEN = 6144
N_MOD = 9
NORM_EPS = 1e-6

LANES = 128
SUBLANES = 8
VMEM_LIMIT = 56 * 1024 * 1024

A_PAD = 4224
C_PAD = 1664
QH = 256
N_PAIRS = A_HEADS // 2
WKV_W = 256
N_QUADS = A_WIDTH // WKV_W
WKV_T = 128
FFT_N2 = 128
ATTN_SUB = 256


def _cparams(sem):
    return pltpu.CompilerParams(dimension_semantics=sem, vmem_limit_bytes=VMEM_LIMIT)


def _pick(n, cands):
    for c in cands:
        if n % c == 0:
            return c
    raise ValueError(f"no tile for {n}")


def _dot(a, b):
    return jnp.dot(a, b, preferred_element_type=F32)


def _split(a):
    hi = a.astype(BF16)
    lo = (a - hi.astype(F32)).astype(BF16)
    return hi, lo


def _dot3(a, b):
    ah, al = _split(a)
    bh, bl = _split(b)
    return _dot(ah, bh) + _dot(al, bh) + _dot(ah, bl)


def _dot2(a, b_bf16):
    ah, al = _split(a)
    return _dot(ah, b_bf16) + _dot(al, b_bf16)


def _sigmoid(x):
    return 1.0 / (1.0 + jnp.exp(-x))


def _silu(x):
    return x * _sigmoid(x)


def _softplus(x):
    return jnp.maximum(x, 0.0) + jnp.log(1.0 + jnp.exp(-jnp.abs(x)))


def _mm_plain_kernel(a_ref, b_ref, o_ref, acc_ref):
    k = pl.program_id(2)

    @pl.when(k == 0)
    def _():
        acc_ref[...] = jnp.zeros_like(acc_ref)

    acc_ref[...] += _dot(a_ref[...], b_ref[...])

    @pl.when(k == pl.num_programs(2) - 1)
    def _():
        o_ref[...] = acc_ref[...].astype(o_ref.dtype)


def _mm_bias_kernel(a_ref, b_ref, bias_ref, o_ref, acc_ref):
    k = pl.program_id(2)

    @pl.when(k == 0)
    def _():
        acc_ref[...] = jnp.zeros_like(acc_ref)

    acc_ref[...] += _dot(a_ref[...], b_ref[...])

    @pl.when(k == pl.num_programs(2) - 1)
    def _():
        o_ref[...] = (acc_ref[...] + bias_ref[...]).astype(o_ref.dtype)


def _mm_swiglu_kernel(a_ref, bg_ref, bu_ref, o_ref, accg_ref, accu_ref):
    k = pl.program_id(2)

    @pl.when(k == 0)
    def _():
        accg_ref[...] = jnp.zeros_like(accg_ref)
        accu_ref[...] = jnp.zeros_like(accu_ref)

    a = a_ref[...]
    accg_ref[...] += _dot(a, bg_ref[...])
    accu_ref[...] += _dot(a, bu_ref[...])

    @pl.when(k == pl.num_programs(2) - 1)
    def _():
        o_ref[...] = (_silu(accg_ref[...]) * accu_ref[...]).astype(o_ref.dtype)


def _mm_resid_kernel(a_ref, b_ref, res_ref, gate_ref, o_ref, acc_ref, *, coef, n_lat, tm):
    k = pl.program_id(2)

    @pl.when(k == 0)
    def _():
        acc_ref[...] = jnp.zeros_like(acc_ref)

    acc_ref[...] += _dot(a_ref[...], b_ref[...])

    @pl.when(k == pl.num_programs(2) - 1)
    def _():
        rows = pl.program_id(0) * tm + lax.broadcasted_iota(jnp.int32, acc_ref.shape, 0)
        gate = jnp.where(rows < n_lat, gate_ref[0:1, :], gate_ref[1:2, :])
        o_ref[...] = res_ref[...] + coef * gate * acc_ref[...]


def _mm(a, b, *, mode="plain", out_dtype=F32, tn=None, tk=None, bias=None, res=None, gate=None,
        coef=1.0, n_lat=0):
    M, K = a.shape
    N = b.shape[1] // 2 if mode == "swiglu" else b.shape[1]
    tm = _pick(M, (1408, 1024, 768, 640, 512, 256, 128, 16))
    tn = tn or _pick(N, (1024, 768, 512, 384, 256, 128))
    tk = tk or _pick(K, (1024, 512, 256, 128))
    grid = (M // tm, N // tn, K // tk)
    a_spec = pl.BlockSpec((tm, tk), lambda i, j, k: (i, k))
    b_spec = pl.BlockSpec((tk, tn), lambda i, j, k: (k, j))
    o_spec = pl.BlockSpec((tm, tn), lambda i, j, k: (i, j))
    acc = pltpu.VMEM((tm, tn), F32)
    if mode == "plain":
        kern, in_specs, args, scratch = _mm_plain_kernel, [a_spec, b_spec], (a, b), [acc]
    elif mode == "bias":
        kern = _mm_bias_kernel
        in_specs = [a_spec, b_spec, pl.BlockSpec((1, tn), lambda i, j, k: (0, j))]
        args, scratch = (a, b, bias), [acc]
    elif mode == "swiglu":
        nj = N // tn
        kern = _mm_swiglu_kernel
        in_specs = [a_spec, b_spec, pl.BlockSpec((tk, tn), lambda i, j, k: (k, j + nj))]
        args, scratch = (a, b, b), [acc, acc]
    else:
        kern = functools.partial(_mm_resid_kernel, coef=coef, n_lat=n_lat, tm=tm)
        in_specs = [a_spec, b_spec, o_spec, pl.BlockSpec((2, tn), lambda i, j, k: (0, j))]
        args, scratch = (a, b, res, gate), [acc]
    return pl.pallas_call(
        kern,
        grid=grid,
        in_specs=in_specs,
        out_specs=o_spec,
        out_shape=jax.ShapeDtypeStruct((M, N), out_dtype),
        scratch_shapes=scratch,
        compiler_params=_cparams(("parallel", "parallel", "arbitrary")),
        name=f"mm_{mode}",
    )(*args)


def _rowwise(fn, toks, consts, outs, tm, *, halo=False, name="rowwise"):
    M = toks[0].shape[0]
    n_t, n_c = len(toks), len(consts)

    def kern(*refs):
        i = pl.program_id(0)
        vals = [r[...] for r in refs[:n_t + (2 if halo else 0) + n_c]]
        res = fn(i, *vals)
        for r, v in zip(refs[n_t + (2 if halo else 0) + n_c:], res):
            r[...] = v.astype(r.dtype)

    in_specs = [pl.BlockSpec((tm, t.shape[1]), lambda i: (i, 0)) for t in toks]
    args = list(toks)
    if halo:
        width = toks[0].shape[1]
        per, last = tm // SUBLANES, M // SUBLANES - 1
        in_specs.append(pl.BlockSpec((SUBLANES, width), lambda i: (jnp.maximum(i * per - 1, 0), 0)))
        in_specs.append(pl.BlockSpec((SUBLANES, width), lambda i: (jnp.minimum((i + 1) * per, last), 0)))
        args += [toks[0], toks[0]]
    for c in consts:
        in_specs.append(pl.BlockSpec(c.shape, lambda i, nd=c.ndim: (0,) * nd))
        args.append(c)
    return pl.pallas_call(
        kern,
        grid=(M // tm,),
        in_specs=in_specs,
        out_specs=[pl.BlockSpec((tm, w), lambda i: (i, 0)) for w, _ in outs],
        out_shape=[jax.ShapeDtypeStruct((M, w), dt) for w, dt in outs],
        compiler_params=_cparams(("parallel",)),
        name=name,
    )(*args)


def _row_ids(i, tm, shape):
    return i * tm + lax.broadcasted_iota(jnp.int32, shape, 0)


def _conv3(i, x, prev8, next8, w, tm, n_lat, n_all):
    rows = _row_ids(i, tm, x.shape)
    local = lax.broadcasted_iota(jnp.int32, x.shape, 0)
    prev_row = jnp.broadcast_to(prev8[SUBLANES - 1:SUBLANES, :], x.shape)
    next_row = jnp.broadcast_to(next8[0:1, :], x.shape)
    xp = jnp.where(local == 0, prev_row, pltpu.roll(x, 1, 0))
    xn = jnp.where(local == tm - 1, next_row, pltpu.roll(x, tm - 1, 0))
    xp = jnp.where((rows == 0) | (rows == n_lat), 0.0, xp)
    xn = jnp.where((rows == n_lat - 1) | (rows == n_all - 1), 0.0, xn)
    return xp * w[0:1, :] + x * w[1:2, :] + xn * w[2:3, :]


def _rms(x, gain, width):
    ms = jnp.sum(x * x, axis=-1, keepdims=True) * (1.0 / width)
    return x * lax.rsqrt(ms + NORM_EPS) * gain


def _wkv_kernel(rf, vf, kkf, wf, kf, bf, rb, vb, kkb, wb, kb, bb, eh_ref, dm_ref,
                yf_ref, yb_ref, s_ref, *, T):
    @pl.when(pl.program_id(0) == 0)
    def _():
        s_ref[...] = jnp.zeros_like(s_ref)

    eh = eh_ref[...]
    dm = dm_ref[...]
    tile = (A_HEAD_DIM, WKV_W)
    dirs = ((rf, vf, kkf, wf, kf, bf, yf_ref), (rb, vb, kkb, wb, kb, bb, yb_ref))
    lane_in_head = lax.broadcasted_iota(jnp.int32, tile, 1) % A_HEAD_DIM
    nq, hpq = N_QUADS, WKV_W // A_HEAD_DIM

    def bc(tile8, j, q):
        return jnp.broadcast_to(tile8[j:j + 1, q * WKV_W:(q + 1) * WKV_W], tile)

    def group(g, carry):
        bases = (pl.multiple_of(g * SUBLANES, SUBLANES), pl.multiple_of(T - (g + 1) * SUBLANES, SUBLANES))
        blk = []
        for d, (r_, v_, kk_, w_, k_, b_, _) in enumerate(dirs):
            rows8 = pl.ds(bases[d], SUBLANES)
            blk.append(dict(r=r_[rows8, :], v=v_[rows8, :].astype(BF16), kk=kk_[rows8, :],
                            w=w_[rows8, :], k=k_[rows8, :], b=b_[rows8, :]))
        vk = []
        for d in range(2):
            lhs = [bc(blk[d]["v"], jj, q) * dm for jj in range(SUBLANES) for q in range(nq)]
            vc = _dot(jnp.concatenate(lhs, axis=0), eh)
            vk.append([[vc[(jj * nq + q) * A_HEAD_DIM:(jj * nq + q + 1) * A_HEAD_DIM] * bc(blk[d]["k"], jj, q)
                        for q in range(nq)] for jj in range(SUBLANES)])
        def emit_y(d, row, ysum):
            z = ysum[:A_HEAD_DIM]
            for q in range(1, nq):
                z = jnp.where(lane_in_head == q, ysum[q * A_HEAD_DIM:(q + 1) * A_HEAD_DIM], z)
            zt = jnp.concatenate([z, jnp.zeros_like(z)], axis=0).T
            for h in range(hpq):
                dirs[d][6][bases[d] + row, h] = zt[h * A_HEAD_DIM:h * A_HEAD_DIM + SUBLANES, :]

        for j in range(SUBLANES):
            js = (j, SUBLANES - 1 - j)
            sas = []
            for d in range(2):
                lhs = [(s_ref[d, q] * bc(blk[d]["kk"], js[d], q)).astype(BF16) for q in range(nq)]
                sas.append(_dot(jnp.concatenate(lhs, axis=0), eh))
            for d in range(2):
                qs = []
                for q in range(nq):
                    sa = sas[d][q * A_HEAD_DIM:(q + 1) * A_HEAD_DIM]
                    s = (s_ref[d, q] * bc(blk[d]["w"], js[d], q) - sa * bc(blk[d]["b"], js[d], q)
                         + vk[d][js[d]][q])
                    s_ref[d, q] = s
                    qs.append((s * bc(blk[d]["r"], js[d], q)).astype(BF16))
                emit_y(d, js[d], _dot(jnp.concatenate(qs, axis=0), eh))
        return carry

    lax.fori_loop(0, T // SUBLANES, group, 0)


def _wkv(r, v, kk, w0, k0, b0, w1, k1, b1, eh, dm, n_lat):
    M = r.shape[0]
    T = WKV_T
    nx, nb = n_lat // T, M // T
    nc = nb - nx

    def fwd(i):
        return (jnp.where(i < nc, nx + i, i - nc), 0)

    def bwd(i):
        return (jnp.where(i < nc, nx + (nc - 1 - i), nx - 1 - (i - nc)), 0)

    fspec = pl.BlockSpec((T, A_WIDTH), fwd)
    bspec = pl.BlockSpec((T, A_WIDTH), bwd)
    hpq = WKV_W // A_HEAD_DIM
    yshape = (T, hpq, SUBLANES, LANES)
    ys = pl.pallas_call(
        functools.partial(_wkv_kernel, T=T),
        grid=(nb,),
        in_specs=[fspec] * 6 + [bspec] * 6 + [
            pl.BlockSpec(c.shape, lambda i: (0, 0)) for c in (eh, dm)],
        out_specs=[pl.BlockSpec(yshape, lambda i: fwd(i) + (0, 0)),
                   pl.BlockSpec(yshape, lambda i: bwd(i) + (0, 0))],
        out_shape=[jax.ShapeDtypeStruct((M,) + yshape[1:], F32)] * 2,
        scratch_shapes=[pltpu.VMEM((2, N_QUADS, A_HEAD_DIM, WKV_W), F32)],
        compiler_params=_cparams(("arbitrary",)),
        name="wkv7_scan",
    )(r, v, kk, w0, k0, b0, r, v, kk, w1, k1, b1, eh, dm)
    return [y[:, :, :N_QUADS, :A_HEAD_DIM].transpose(0, 2, 1, 3).reshape(M, A_WIDTH) for y in ys]


def _dft_rows_kernel(fc_ref, fs_ref, x_ref, ar_ref, ai_ref):
    x = x_ref[...].astype(BF16)
    ar_ref[...] = _dot(fc_ref[...], x)
    ai_ref[...] = -_dot(fs_ref[...], x)


def _dft_rows(fc, fs, x2d):
    W = x2d.shape[1]
    n1, n1h = fc.shape
    tn = _pick(W, (4096, 2048, 1024, 512, 256, 128))
    fspec = pl.BlockSpec(fc.shape, lambda j: (0, 0))
    return pl.pallas_call(
        _dft_rows_kernel,
        grid=(W // tn,),
        in_specs=[fspec, fspec, pl.BlockSpec((n1h, tn), lambda j: (0, j))],
        out_specs=[pl.BlockSpec((n1, tn), lambda j: (0, j))] * 2,
        out_shape=[jax.ShapeDtypeStruct((n1, W), F32)] * 2,
        compiler_params=_cparams(("parallel",)),
        name="fft_stage1",
    )(fc, fs, x2d)


def _tile_lanes(t, width):
    return t if width == LANES else jnp.concatenate([t] * (width // LANES), axis=-1)


def _cdot(m, xr, xi):
    out = _dot(m, jnp.concatenate([xr, xi], axis=0).astype(BF16))
    return out[:FFT_N2], out[FFT_N2:]


def _fft_fwd_mid(ar, ai, twr, twi, mf):
    return _cdot(mf, ar * twr - ai * twi, ar * twi + ai * twr)


def _filter_spec_kernel(afr, afi, abr, abi, twr_ref, twi_ref, shr_ref, shi_ref, mf_ref, hr_ref, hi_ref):
    ct = afr.shape[-1]
    twr, twi = _tile_lanes(twr_ref[0], ct), _tile_lanes(twi_ref[0], ct)
    shr, shi = _tile_lanes(shr_ref[0], ct), _tile_lanes(shi_ref[0], ct)
    mf = mf_ref[...]
    fr, fi = _fft_fwd_mid(afr[0], afi[0], twr, twi, mf)
    br, bi = _fft_fwd_mid(abr[0], abi[0], twr, twi, mf)
    hr_ref[0] = fr + br * shr + bi * shi
    hi_ref[0] = fi + br * shi - bi * shr


def _conv_spec_kernel(agr, agi, hr, hi, twr_ref, twi_ref, mf_ref, mi_ref, dr_ref, di_ref):
    ct = agr.shape[-1]
    twr, twi = _tile_lanes(twr_ref[0], ct), _tile_lanes(twi_ref[0], ct)
    xr, xi = _fft_fwd_mid(agr[0], agi[0], twr, twi, mf_ref[...])
    yr = xr * hr[0] - xi * hi[0]
    yi = xr * hi[0] + xi * hr[0]
    cr, ci = _cdot(mi_ref[...], yr, yi)
    dr_ref[0] = cr * twr + ci * twi
    di_ref[0] = ci * twr - cr * twi


def _idft_rows_kernel(gc_ref, gs_ref, dr_ref, di_ref, y_ref, *, scale):
    y_ref[...] = scale * (_dot(gc_ref[...], dr_ref[...].astype(BF16))
                          - _dot(gs_ref[...], di_ref[...].astype(BF16)))


def _idft_rows(gc, gs, dr2d, di2d, scale):
    n1, W = dr2d.shape
    n1h = gc.shape[0]
    tn = _pick(W, (4096, 2048, 1024, 512, 256, 128))
    gspec = pl.BlockSpec(gc.shape, lambda j: (0, 0))
    dspec = pl.BlockSpec((n1, tn), lambda j: (0, j))
    return pl.pallas_call(
        functools.partial(_idft_rows_kernel, scale=scale),
        grid=(W // tn,),
        in_specs=[gspec, gspec, dspec, dspec],
        out_specs=pl.BlockSpec((n1h, tn), lambda j: (0, j)),
        out_shape=jax.ShapeDtypeStruct((n1h, W), F32),
        compiler_params=_cparams(("parallel",)),
        name="fft_stage1_inv",
    )(gc, gs, dr2d, di2d)


def _fft_tables(n_lat):
    n = 2 * n_lat
    n1 = n // FFT_N2

    def cs(rows, cols, period):
        idx = (jnp.arange(rows, dtype=jnp.int32)[:, None] * jnp.arange(cols, dtype=jnp.int32)[None, :]) % period
        ang = idx.astype(F32) * (2.0 * math.pi / period)
        return jnp.cos(ang), jnp.sin(ang)

    n1h = n1 // 2
    nk = n1h + 1
    nkp = -(-nk // SUBLANES) * SUBLANES
    live = (jnp.arange(nkp) < nk).astype(F32)[:, None]
    f1c, f1s = cs(nkp, n1h, n1)
    f1c, f1s = f1c * live, f1s * live
    f2c, f2s = cs(FFT_N2, FFT_N2, FFT_N2)
    twc, tws = cs(nkp, FFT_N2, n)
    shape = (nkp, FFT_N2, LANES)
    twc = jnp.broadcast_to(twc[:, :, None], shape)
    tws = jnp.broadcast_to(tws[:, :, None], shape)
    kidx = (jnp.arange(nkp, dtype=jnp.int32)[:, None] + n1 * jnp.arange(FFT_N2, dtype=jnp.int32)[None, :]) % n
    kang = kidx.astype(F32) * (2.0 * math.pi / n)
    shr = jnp.broadcast_to(jnp.cos(kang)[:, :, None], shape)
    shi = jnp.broadcast_to(jnp.sin(kang)[:, :, None], shape)
    gc, gs = cs(n1h, nkp, n1)
    k1 = jnp.arange(nkp)
    ck = jnp.where((k1 == 0) | (k1 == n1h), 1.0, 2.0) * (k1 < nk)
    gc, gs = gc * ck[None, :], gs * ck[None, :]

    mf = jnp.block([[f2c, f2s], [-f2s, f2c]]).astype(BF16)
    mi = jnp.block([[f2c, -f2s], [f2s, f2c]]).astype(BF16)
    return dict(n=n, n1=n1, nkp=nkp, f1c=f1c.astype(BF16), f1s=f1s.astype(BF16),
                gc=gc.astype(BF16), gs=gs.astype(BF16), mf=mf, mi=mi,
                twr=twc, twi=-tws, shr=shr, shi=shi)


def _long_conv(g, h, n_lat, tabs):
    M, C = g.shape
    n, nkp = tabs["n"], tabs["nkp"]
    aqr, aqi = _dft_rows(tabs["f1c"], tabs["f1s"], h.reshape(n_lat // FFT_N2, FFT_N2 * 2 * C))
    aqr = aqr.reshape(nkp, FFT_N2, 2 * C)
    aqi = aqi.reshape(nkp, FFT_N2, 2 * C)
    ct = _pick(C, (512, 256, 128))
    nj = C // ct
    blk_f = pl.BlockSpec((1, FFT_N2, ct), lambda k, j: (k, 0, j))
    blk_p = pl.BlockSpec((1, FFT_N2, ct), lambda k, j: (k, 0, j + nj))
    tw = pl.BlockSpec((1, FFT_N2, LANES), lambda k, j: (k, 0, 0))
    fm = pl.BlockSpec(tabs["mf"].shape, lambda k, j: (0, 0))
    hr, hi = pl.pallas_call(
        _filter_spec_kernel,
        grid=(nkp, nj),
        in_specs=[blk_f, blk_f, blk_p, blk_p, tw, tw, tw, tw, fm],
        out_specs=[blk_f, blk_f],
        out_shape=[jax.ShapeDtypeStruct((nkp, FFT_N2, C), F32)] * 2,
        compiler_params=_cparams(("parallel", "parallel")),
        name="fft_filter_spectrum",
    )(aqr, aqi, aqr, aqi, tabs["twr"], tabs["twi"], tabs["shr"], tabs["shi"], tabs["mf"])
    agr, agi = _dft_rows(tabs["f1c"], tabs["f1s"], g.reshape(M // FFT_N2, FFT_N2 * C))
    agr = agr.reshape(nkp, FFT_N2, C)
    agi = agi.reshape(nkp, FFT_N2, C)
    dr, di = pl.pallas_call(
        _conv_spec_kernel,
        grid=(nkp, nj),
        in_specs=[blk_f] * 4 + [tw, tw, fm, fm],
        out_specs=[blk_f, blk_f],
        out_shape=[jax.ShapeDtypeStruct((nkp, FFT_N2, C), F32)] * 2,
        compiler_params=_cparams(("parallel", "parallel")),
        name="fft_conv_spectrum",
    )(agr, agi, hr, hi, tabs["twr"], tabs["twi"], tabs["mf"], tabs["mi"])
    y2d = _idft_rows(tabs["gc"], tabs["gs"], dr.reshape(nkp, FFT_N2 * C), di.reshape(nkp, FFT_N2 * C),
                     1.0 / n)
    return y2d.reshape(n_lat, C)


def _ctx_conv_kernel(g_ref, hf_ref, hb_ref, y_ref, *, n):
    g = g_ref[...]
    rows = lax.broadcasted_iota(jnp.int32, g.shape, 0)
    y_ref[...] = jnp.zeros_like(y_ref)

    def causal(d, carry):
        y_ref[...] += jnp.where(rows >= d, pltpu.roll(g, d, 0), 0.0) * hf_ref[pl.ds(d, 1), :]
        return carry

    def anticausal(d, carry):
        y_ref[...] += jnp.where(rows < n - d, pltpu.roll(g, n - d, 0), 0.0) * hb_ref[pl.ds(d - 1, 1), :]
        return carry

    lax.fori_loop(0, n, causal, 0)
    lax.fori_loop(1, n, anticausal, 0)


def _ctx_conv(g, h, n_lat, n_ctx):
    C = g.shape[1]
    ct = _pick(C, (256, 128))
    nj = C // ct
    rb = n_lat // n_ctx
    return pl.pallas_call(
        functools.partial(_ctx_conv_kernel, n=n_ctx),
        grid=(nj,),
        in_specs=[pl.BlockSpec((n_ctx, ct), lambda j: (rb, j)),
                  pl.BlockSpec((n_ctx, ct), lambda j: (0, j)),
                  pl.BlockSpec((n_ctx, ct), lambda j: (0, j + nj))],
        out_specs=pl.BlockSpec((n_ctx, ct), lambda j: (0, j)),
        out_shape=jax.ShapeDtypeStruct((n_ctx, C), F32),
        compiler_params=_cparams(("parallel",)),
        name="hyena_ctx_conv",
    )(g, h, h)


def _filter_feats(length):
    t = jnp.linspace(0.0, 1.0, length, dtype=F32)[:, None]
    w = (2.0 * math.pi / length) * jnp.arange(length, dtype=F32)[:, None]
    f = jnp.linspace(1e-4, FILTER_BANDS - 1, FILTER_BANDS, dtype=F32)[None, :]
    z = jnp.concatenate([t, jnp.cos(f * w), -jnp.sin(f * w)], axis=-1)
    return jnp.pad(z, ((0, 0), (0, LANES - FILTER_EMB)))


def _filter_kernel(z_ref, w1, b1, w2, b2, w3, b3, w4, fr, dl, h_ref, s_ref, *, tm, length):
    i = pl.program_id(0)
    z = z_ref[...]
    freq = fr[...]
    h = jnp.sin(freq * (_dot3(z, w1[...]) + b1[...]))
    h = jnp.sin(freq * (_dot3(h, w2[...]) + b2[...]))
    h = jnp.sin(freq * (_dot3(h, w3[...]) + b3[...]))
    h = _dot3(h, w4[...])
    h = h * jnp.exp(-z[:, 0:1] * dl[...])
    rows = _row_ids(i, tm, h.shape)
    cols = lax.broadcasted_iota(jnp.int32, h.shape, 1)
    keep = (cols < B_WIDTH) | (rows < length - 1)
    h = jnp.where(keep, h, 0.0)
    h_ref[...] = h
    part = jnp.sum(jnp.abs(h), axis=0, keepdims=True)

    @pl.when(i == 0)
    def _():
        s_ref[...] = jnp.zeros_like(s_ref)

    s_ref[...] += jnp.broadcast_to(part, s_ref.shape)


def _filter_taps(length, fp):
    z = _filter_feats(length)
    tm = _pick(length, (256, 128))
    consts = [fp["w1"], fp["b1"], fp["w2"], fp["b2"], fp["w3"], fp["b3"], fp["w4"], fp["freq"], fp["deltas"]]
    in_specs = [pl.BlockSpec((tm, LANES), lambda i: (i, 0))]
    in_specs += [pl.BlockSpec(c.shape, lambda i: (0, 0)) for c in consts]
    h, s = pl.pallas_call(
        functools.partial(_filter_kernel, tm=tm, length=length),
        grid=(length // tm,),
        in_specs=in_specs,
        out_specs=[pl.BlockSpec((tm, 2 * B_WIDTH), lambda i: (i, 0)),
                   pl.BlockSpec((SUBLANES, 2 * B_WIDTH), lambda i: (0, 0))],
        out_shape=[jax.ShapeDtypeStruct((length, 2 * B_WIDTH), F32),
                   jax.ShapeDtypeStruct((SUBLANES, 2 * B_WIDTH), F32)],
        compiler_params=_cparams(("arbitrary",)),
        name="hyena_filter",
    )(z, *consts)
    return h, s


def _attn_kernel(q_ref, k_ref, v_ref, o_ref, m_ref, acc_ref, *, tk):
    m_ref[...] = jnp.full_like(m_ref, -jnp.inf)
    acc_ref[...] = jnp.zeros_like(acc_ref)
    for c0 in range(0, k_ref.shape[0], tk):
        k, v = k_ref[c0:c0 + tk, :], v_ref[c0:c0 + tk, :]
        for r0 in range(0, q_ref.shape[0], ATTN_SUB):
            rs = slice(r0, r0 + ATTN_SUB)
            s = lax.dot_general(q_ref[rs, :], k, (((1,), (1,)), ((), ())), preferred_element_type=F32)
            m_old = m_ref[rs, :]
            m_new = jnp.maximum(m_old, jnp.max(s, axis=-1, keepdims=True))
            alpha = jnp.exp(m_old - m_new)
            p = jnp.exp(s - m_new)
            acc_ref[rs, :] = alpha * acc_ref[rs, :] + _dot(p.astype(BF16), v)
            m_ref[rs, :] = m_new
    acc = acc_ref[...]
    o_ref[...] = (acc[:, :V_HEAD] / acc[:, V_HEAD:V_HEAD + 1]).astype(o_ref.dtype)


def _attention(q, k, v, q_row0, n_q, k_row0, n_k):
    tq = _pick(n_q, (1024, 512, 256))
    tk = _pick(n_k, (768, 1024, 640, 512, 256))
    assert q_row0 % tq == 0 and k_row0 % n_k == 0
    qb, kb = q_row0 // tq, k_row0 // n_k
    return pl.pallas_call(
        functools.partial(_attn_kernel, tk=tk),
        grid=(C_HEADS, n_q // tq),
        in_specs=[pl.BlockSpec((tq, QH), lambda h, i: (qb + i, h)),
                  pl.BlockSpec((n_k, QH), lambda h, i: (kb, h)),
                  pl.BlockSpec((n_k, QH), lambda h, i: (kb, h))],
        out_specs=pl.BlockSpec((tq, V_HEAD), lambda h, i: (i, h)),
        out_shape=jax.ShapeDtypeStruct((n_q, C_WIDTH), BF16),
        scratch_shapes=[pltpu.VMEM((tq, 1), F32), pltpu.VMEM((tq, QH), F32)],
        compiler_params=_cparams(("parallel", "parallel")),
        name="mla_attention",
    )(q, k, v)


def _pair_sum(x, eh):
    return jnp.concatenate(
        [_dot2(x[:, p * LANES:(p + 1) * LANES], eh) for p in range(N_PAIRS)], axis=-1)


def _rot_half(t):
    lane = lax.broadcasted_iota(jnp.int32, t.shape, 1)
    up = pltpu.roll(t, LANES - 16, 1)
    down = pltpu.roll(t, 16, 1)
    return jnp.where(lane % 32 < 16, -up, down)


def _layer(xa, mod, p, const, n_lat, need_ctx):
    M = xa.shape[0]
    eh = const["eh"]

    def modrow(k):
        return mod[:, k * D_MODEL:(k + 1) * D_MODEL]

    def norm_mod(x, gain, shift, scale):
        tm = 256

        def fn(i, xb, g, sh, sc):
            rows = _row_ids(i, tm, xb.shape)
            lat = rows < n_lat
            y = _rms(xb, g, D_MODEL)
            return (y * (1.0 + jnp.where(lat, sc[0:1], sc[1:2])) + jnp.where(lat, sh[0:1], sh[1:2]),)

        return _rowwise(fn, [x], [gain, shift, scale], [(D_MODEL, BF16)], tm, name="norm_modulate")[0]

    def ffn(x, gain, w_gu, w_down, k0):
        h = norm_mod(x, gain, modrow(k0), modrow(k0 + 1))
        act = _mm(h, w_gu, mode="swiglu", out_dtype=BF16, tn=1024)
        return _mm(act, w_down, mode="resid", res=x, gate=modrow(k0 + 2), coef=0.5, n_lat=n_lat)

    xa = ffn(xa, p["norm_ffn1"], p["ffn1_gu"], p["ffn1_down"], 0)

    h = norm_mod(xa, p["norm_mix"], modrow(3), modrow(4))
    pa = _mm(h, p["w_in_a"], tn=1408)
    pb = _mm(h, p["w_in_b"], tn=1280)
    pc = _mm(h, p["w_in_c"], tn=C_PAD)

    tma = 128

    def rwkv_prep(i, x, prev8, next8, cw, k_k, k_a, w0, wup0, wup1, a0, aup0, aup1, gup, eh_):
        u = _conv3(i, x, prev8, next8, cw, tma, n_lat, M)
        r, k, v = u[:, :A_WIDTH], u[:, A_WIDTH:2 * A_WIDTH], u[:, 2 * A_WIDTH:3 * A_WIDTH]
        xwa = u[:, 3 * A_WIDTH:3 * A_WIDTH + LANES]
        xg = u[:, 3 * A_WIDTH + LANES:3 * A_WIDTH + 3 * LANES]
        kk = k * k_k
        kk = kk / jnp.maximum(jnp.sqrt(_pair_sum(kk * kk, eh_)), 1e-12)
        g = _dot3(_sigmoid(xg), gup)
        th = jnp.tanh(xwa)
        outs = [r, v, kk, g]
        for wup, aup, d in ((wup0, aup0, 0), (wup1, aup1, 1)):
            logw = -_softplus(-(w0[d:d + 1] + _dot3(th, wup))) - 0.5
            decay = jnp.exp(-jnp.exp(logw))
            a = _sigmoid(a0[d:d + 1] + _dot3(xwa, aup))
            outs += [decay, k * (1.0 + (a - 1.0) * k_a), kk * a]
        return outs

    r, v, kk, g, w0, k0, b0, w1, k1, b1 = _rowwise(
        rwkv_prep, [pa],
        [p["a_conv"], p["a_k_k"], p["a_k_a"], p["a_w0"], p["a_w_up0"], p["a_w_up1"], p["a_a0"],
         p["a_a_up0"], p["a_a_up1"], p["a_g_up"], eh],
        [(A_WIDTH, F32)] * 10, tma, halo=True, name="rwkv_prep")
    yf, yb = _wkv(r, v, kk, w0, k0, b0, w1, k1, b1, const["eh2"], const["dm2"], n_lat)

    def rwkv_out(i, yf_, yb_, r_, v_, g_, k0_, k1_, r_k, ln_w, ln_b, eh_):
        y = yf_ + yb_
        inv = 1.0 / A_HEAD_DIM
        mu = _pair_sum(y, eh_) * inv
        yc = y - mu
        var = _pair_sum(yc * yc, eh_) * inv
        yn = yc * lax.rsqrt(var + GN_EPS) * ln_w + ln_b
        bonus = _pair_sum(r_ * (0.5 * (k0_ + k1_)) * r_k, eh_) * v_
        return ((yn + bonus) * g_,)

    o_a = _rowwise(rwkv_out, [yf, yb, r, v, g, k0, k1], [p["a_r_k"], p["a_ln_w"], p["a_ln_b"], eh],
                   [(A_WIDTH, BF16)], 256, name="rwkv_out")[0]

    tmb = 256

    def hyena_prep(i, x, prev8, next8, cw, cb):
        u = _conv3(i, x, prev8, next8, cw, tmb, n_lat, M) + cb
        return u[:, :B_WIDTH], u[:, B_WIDTH:2 * B_WIDTH] * u[:, 2 * B_WIDTH:]

    x0, gsig = _rowwise(hyena_prep, [pb], [p["b_conv"], p["b_conv_b"]],
                        [(B_WIDTH, F32)] * 2, tmb, halo=True, name="hyena_prep")
    n_ctx = M - n_lat
    hx, sx = _filter_taps(n_lat, p["filt"])
    hc, sc = _filter_taps(n_ctx, p["filt"])
    y_lat = _long_conv(gsig, hx, n_lat, const["fft"])
    y_ctx = _ctx_conv(gsig, hc, n_lat, n_ctx) if need_ctx else jnp.zeros((n_ctx, B_WIDTH), F32)
    yconv = jnp.concatenate([y_lat, y_ctx], axis=0)

    def hyena_out(i, x0_, g_, y_, bias, sx_, sc_):
        rows = _row_ids(i, tmb, x0_.shape)
        nx_ = sx_[0:1, :B_WIDTH] + sx_[0:1, B_WIDTH:]
        nc_ = sc_[0:1, :B_WIDTH] + sc_[0:1, B_WIDTH:]
        norm = jnp.where(rows < n_lat, nx_, nc_)
        return (x0_ * (y_ / norm + g_ * bias),)

    o_b = _rowwise(hyena_out, [x0, gsig, yconv], [p["b_bias"], sx, sc], [(B_WIDTH, BF16)], tmb,
                   name="hyena_out")[0]

    tmc = 256

    def mla_prep(i, x, cos, sin, gq, gkv, gpe):
        cq, ckv = x[:, :Q_RANK], x[:, Q_RANK:Q_RANK + KV_RANK]
        kpe = x[:, Q_RANK + KV_RANK:]
        kpe = _rms(kpe, gpe, QK_ROPE)
        kpe = kpe * cos + _rot_half(kpe) * sin
        return _rms(cq, gq, Q_RANK), _rms(ckv, gkv, KV_RANK), kpe

    cqn, ckvn, kpe = _rowwise(mla_prep, [pc, const["cos"], const["sin"]],
                              [p["c_q_norm"], p["c_kv_norm"], p["c_kn_pe"]],
                              [(Q_RANK, BF16), (KV_RANK, BF16), (LANES, F32)], tmc, name="mla_prep")
    q_raw = _mm(cqn, p["c_q_up"], tn=768)
    kv_raw = _mm(ckvn, p["c_kv_up"], tn=768)
    scale = 1.0 / math.sqrt(QK_DIM)

    def mla_heads(i, qr, kvr, kpe_, cos, sin, gn_q, gpe_q, gn_k):
        qs, ks, vs = [], [], []
        for hh in range(C_HEADS):
            qn = _rms(qr[:, hh * QH:hh * QH + QK_NOPE], gn_q, QK_NOPE)
            qp = _rms(qr[:, hh * QH + QK_NOPE:(hh + 1) * QH], gpe_q, QK_ROPE)
            qp = qp * cos + _rot_half(qp) * sin
            qs += [qn * scale, qp * scale]
            ks += [_rms(kvr[:, hh * QH:hh * QH + QK_NOPE], gn_k, QK_NOPE), kpe_]
            vh = kvr[:, hh * QH + QK_NOPE:(hh + 1) * QH]
            vs += [vh, jnp.ones_like(vh)]
        return jnp.concatenate(qs, axis=-1), jnp.concatenate(ks, axis=-1), jnp.concatenate(vs, axis=-1)

    q, k, vv = _rowwise(mla_heads, [q_raw, kv_raw, kpe, const["cos"], const["sin"]],
                        [p["c_qn_nope"], p["c_qn_pe"], p["c_kn_nope"]],
                        [(C_HEADS * QH, BF16)] * 3, tmc, name="mla_heads")
    oc_x = _attention(q, k, vv, 0, n_lat, 0, M)
    if need_ctx:
        oc_c = _attention(q, k, vv, n_lat, n_ctx, n_lat, n_ctx)
    else:
        oc_c = jnp.zeros((n_ctx, C_WIDTH), BF16)
    o_c = jnp.concatenate([oc_x, oc_c], axis=0)

    o = jnp.concatenate([o_a, o_b, o_c], axis=-1)
    xa = _mm(o, p["w_out"], mode="resid", res=xa, gate=modrow(5), coef=1.0, n_lat=n_lat)
    return ffn(xa, p["norm_ffn2"], p["ffn2_gu"], p["ffn2_down"], 6)


def _pad_cols(w, width):
    return jnp.pad(w, [(0, 0)] * (w.ndim - 1) + [(0, width - w.shape[-1])])


def _pad_rows(w, height):
    return jnp.pad(w, [(0, 0)] * (w.ndim - 2) + [(0, height - w.shape[-2]), (0, 0)])


def _row(v):
    return v.reshape(1, -1)


def kernel(x, c, ctx, c_ctx, ada_down, ada_up, ada_bias, norm_ffn1, norm_mix, norm_ffn2, ffn1_gu, ffn1_down, ffn2_gu, ffn2_down, w_in, w_out, a_conv, a_w0, a_w_up, a_a0, a_a_up, a_g_up, a_k_k, a_k_a, a_r_k, a_ln_w, a_ln_b, b_conv, b_conv_b, b_fw1, b_fb1, b_fw2, b_fb2, b_fw3, b_fb3, b_fw4, b_freq, b_bias, c_q_norm, c_q_up, c_kv_norm, c_kv_up, c_qn_nope, c_qn_pe, c_kn_nope, c_kn_pe):
    n_lat, n_ctx = x.shape[1], ctx.shape[1]
    depth = w_in.shape[0]
    xa = jnp.concatenate([x[0], ctx[0]], axis=0)

    lane = jnp.arange(LANES)
    eh = (lane[:, None] // A_HEAD_DIM == lane[None, :] // A_HEAD_DIM).astype(BF16)
    lane2 = jnp.arange(WKV_W)
    eh2 = (lane2[:, None] // A_HEAD_DIM == lane2[None, :] // A_HEAD_DIM).astype(BF16)
    dm2 = (lane2[None, :] % A_HEAD_DIM == jnp.arange(A_HEAD_DIM)[:, None]).astype(BF16)
    rows_g = n_lat // GRID_W
    row = jnp.broadcast_to(jnp.arange(rows_g, dtype=F32)[:, None], (rows_g, GRID_W)).reshape(-1)
    col = jnp.broadcast_to(jnp.arange(GRID_W, dtype=F32)[None, :], (rows_g, GRID_W)).reshape(-1)
    half = QK_ROPE // 2
    inv = ROPE_THETA ** (-jnp.arange(0, half, 2, dtype=F32) / half)
    ang = jnp.concatenate([row[:, None] * inv, row[:, None] * inv, col[:, None] * inv, col[:, None] * inv], axis=-1)
    cos = jnp.concatenate([jnp.cos(ang), jnp.ones((n_ctx, QK_ROPE), F32)], axis=0)
    sin = jnp.concatenate([jnp.sin(ang), jnp.zeros((n_ctx, QK_ROPE), F32)], axis=0)
    const = dict(eh=eh, eh2=eh2, dm2=dm2, cos=_pad_cols(cos, LANES), sin=_pad_cols(sin, LANES), fft=_fft_tables(n_lat))
    deltas = jnp.abs(jnp.linspace(MIN_DECAY, MAX_DECAY, B_WIDTH, dtype=F32))

    cc = jnp.pad(jnp.concatenate([c, c_ctx[None, :]], axis=0), ((0, 14), (0, 0)))
    cs = _rowwise(lambda i, t: (_silu(t),), [cc], [], [(D_MODEL, BF16)], 16, name="silu")[0]

    q_up = c_q_up.reshape(depth, Q_RANK, C_HEADS, QK_DIM)
    q_up = jnp.pad(q_up, ((0, 0), (0, 0), (0, 0), (0, QH - QK_DIM))).reshape(depth, Q_RANK, C_HEADS * QH)

    for l in range(depth):
        mid = _mm(cs, ada_down[l].astype(BF16), out_dtype=BF16, tn=256)
        mod = _mm(mid, ada_up[l].astype(BF16), mode="bias", bias=_row(ada_bias[l]), tn=1024, tk=256)[:2]
        w_in_l = w_in[l]
        filt = dict(
            w1=_pad_cols(_pad_rows(b_fw1[l], LANES), LANES), b1=_pad_cols(_row(b_fb1[l]), LANES),
            w2=_pad_cols(_pad_rows(b_fw2[l], LANES), LANES), b2=_pad_cols(_row(b_fb2[l]), LANES),
            w3=_pad_cols(_pad_rows(b_fw3[l], LANES), LANES), b3=_pad_cols(_row(b_fb3[l]), LANES),
            w4=_pad_rows(b_fw4[l], LANES), freq=_pad_cols(_row(b_freq[l]), LANES),
            deltas=_row(jnp.tile(deltas, 2)))
        zpad = jnp.zeros((A_DECAY_RANK, A_WIDTH), F32)
        p = dict(
            norm_ffn1=_row(norm_ffn1[l]), norm_mix=_row(norm_mix[l]), norm_ffn2=_row(norm_ffn2[l]),
            ffn1_gu=ffn1_gu[l].astype(BF16), ffn1_down=ffn1_down[l].astype(BF16),
            ffn2_gu=ffn2_gu[l].astype(BF16), ffn2_down=ffn2_down[l].astype(BF16),
            w_in_a=_pad_cols(w_in_l[:, :A_IN], A_PAD).astype(BF16),
            w_in_b=w_in_l[:, A_IN:A_IN + B_IN].astype(BF16),
            w_in_c=_pad_cols(w_in_l[:, A_IN + B_IN:], C_PAD).astype(BF16),
            w_out=w_out[l].astype(BF16),
            a_conv=_pad_cols(a_conv[l], A_PAD), a_k_k=_row(a_k_k[l]), a_k_a=_row(a_k_a[l]),
            a_w0=a_w0[l], a_a0=a_a0[l],
            a_w_up0=jnp.concatenate([a_w_up[l, 0], zpad], axis=0),
            a_w_up1=jnp.concatenate([a_w_up[l, 1], zpad], axis=0),
            a_a_up0=jnp.concatenate([zpad, a_a_up[l, 0]], axis=0),
            a_a_up1=jnp.concatenate([zpad, a_a_up[l, 1]], axis=0),
            a_g_up=_pad_rows(a_g_up[l], 2 * LANES),
            a_r_k=_row(a_r_k[l]), a_ln_w=_row(a_ln_w[l]), a_ln_b=_row(a_ln_b[l]),
            b_conv=b_conv[l], b_conv_b=_row(b_conv_b[l]), b_bias=_row(b_bias[l]), filt=filt,
            c_q_norm=_row(c_q_norm[l]), c_kv_norm=_row(c_kv_norm[l]),
            c_kn_pe=_pad_cols(_row(c_kn_pe[l]), LANES),
            c_q_up=q_up[l].astype(BF16), c_kv_up=c_kv_up[l].astype(BF16),
            c_qn_nope=_row(c_qn_nope[l]), c_qn_pe=_pad_cols(_row(c_qn_pe[l]), LANES),
            c_kn_nope=_row(c_kn_nope[l]),
        )
        xa = _layer(xa, mod, p, const, n_lat, l < depth - 1)
    return xa[:n_lat][None]
```

```python
import functools
import math

import jax
import jax.numpy as jnp
from jax import lax
from jax.experimental import pallas as pl
from jax.experimental.pallas import tpu as pltpu

F32 = jnp.float32
BF16 = jnp.bfloat16

D_MODEL = 4096
GRID_W = 64
A_HEADS = 20
A_HEAD_DIM = 64
A_WIDTH = A_HEADS * A_HEAD_DIM
A_DECAY_RANK = 64
A_ICLR_RANK = 64
A_GATE_RANK = 192
A_IN = 3 * A_WIDTH + A_DECAY_RANK + A_ICLR_RANK + A_GATE_RANK
GN_EPS = 64e-5
B_WIDTH = 1280
B_IN = 3 * B_WIDTH
FILTER_EMB = 33
FILTER_BANDS = (FILTER_EMB - 1) // 2
FILTER_HIDDEN = 64
MIN_DECAY = math.log(1e-2) / 1.5
MAX_DECAY = math.log(1e-2) / 0.3
C_HEADS = 12
QK_NOPE = 128
QK_ROPE = 64
QK_DIM = QK_NOPE + QK_ROPE
V_HEAD = 128
C_WIDTH = C_HEADS * V_HEAD
Q_RANK = 1024
KV_RANK = 512
C_IN = Q_RANK + KV_RANK + QK_ROPE
ROPE_THETA = 10000.0
FFN_HIDDEN = 6144
N_MOD = 9
NORM_EPS = 1e-6

LANES = 128
SUBLANES = 8
VMEM_LIMIT = 56 * 1024 * 1024

A_PAD = 4224
C_PAD = 1664
QH = 256
N_PAIRS = A_HEADS // 2
WKV_W = 256
N_QUADS = A_WIDTH // WKV_W
WKV_T = 128
FFT_N2 = 128
ATTN_SUB = 256


def _cparams(sem):
    return pltpu.CompilerParams(dimension_semantics=sem, vmem_limit_bytes=VMEM_LIMIT)


def _pick(n, cands):
    for c in cands:
        if n % c == 0:
            return c
    raise ValueError(f"no tile for {n}")


def _dot(a, b):
    return jnp.dot(a, b, preferred_element_type=F32)


def _split(a):
    hi = a.astype(BF16)
    lo = (a - hi.astype(F32)).astype(BF16)
    return hi, lo


def _dot3(a, b):
    ah, al = _split(a)
    bh, bl = _split(b)
    return _dot(ah, bh) + _dot(al, bh) + _dot(ah, bl)


def _dot2(a, b_bf16):
    ah, al = _split(a)
    return _dot(ah, b_bf16) + _dot(al, b_bf16)


def _sigmoid(x):
    return 1.0 / (1.0 + jnp.exp(-x))


def _silu(x):
    return x * _sigmoid(x)


def _softplus(x):
    return jnp.maximum(x, 0.0) + jnp.log(1.0 + jnp.exp(-jnp.abs(x)))


def _mm_plain_kernel(a_ref, b_ref, o_ref, acc_ref):
    k = pl.program_id(2)

    @pl.when(k == 0)
    def _():
        acc_ref[...] = jnp.zeros_like(acc_ref)

    acc_ref[...] += _dot(a_ref[...], b_ref[...])

    @pl.when(k == pl.num_programs(2) - 1)
    def _():
        o_ref[...] = acc_ref[...].astype(o_ref.dtype)


def _mm_bias_kernel(a_ref, b_ref, bias_ref, o_ref, acc_ref):
    k = pl.program_id(2)

    @pl.when(k == 0)
    def _():
        acc_ref[...] = jnp.zeros_like(acc_ref)

    acc_ref[...] += _dot(a_ref[...], b_ref[...])

    @pl.when(k == pl.num_programs(2) - 1)
    def _():
        o_ref[...] = (acc_ref[...] + bias_ref[...]).astype(o_ref.dtype)


def _mm_swiglu_kernel(a_ref, bg_ref, bu_ref, o_ref, accg_ref, accu_ref):
    k = pl.program_id(2)

    @pl.when(k == 0)
    def _():
        accg_ref[...] = jnp.zeros_like(accg_ref)
        accu_ref[...] = jnp.zeros_like(accu_ref)

    a = a_ref[...]
    accg_ref[...] += _dot(a, bg_ref[...])
    accu_ref[...] += _dot(a, bu_ref[...])

    @pl.when(k == pl.num_programs(2) - 1)
    def _():
        o_ref[...] = (_silu(accg_ref[...]) * accu_ref[...]).astype(o_ref.dtype)


def _mm_resid_kernel(a_ref, b_ref, res_ref, gate_ref, o_ref, acc_ref, *, coef, n_lat, tm):
    k = pl.program_id(2)

    @pl.when(k == 0)
    def _():
        acc_ref[...] = jnp.zeros_like(acc_ref)

    acc_ref[...] += _dot(a_ref[...], b_ref[...])

    @pl.when(k == pl.num_programs(2) - 1)
    def _():
        rows = pl.program_id(0) * tm + lax.broadcasted_iota(jnp.int32, acc_ref.shape, 0)
        gate = jnp.where(rows < n_lat, gate_ref[0:1, :], gate_ref[1:2, :])
        o_ref[...] = res_ref[...] + coef * gate * acc_ref[...]


def _mm(a, b, *, mode="plain", out_dtype=F32, tn=None, tk=None, bias=None, res=None, gate=None,
        coef=1.0, n_lat=0):
    M, K = a.shape
    N = b.shape[1] // 2 if mode == "swiglu" else b.shape[1]
    tm = _pick(M, (1408, 1024, 768, 640, 512, 256, 128, 16))
    tn = tn or _pick(N, (1024, 768, 512, 384, 256, 128))
    tk = tk or _pick(K, (1024, 512, 256, 128))
    grid = (M // tm, N // tn, K // tk)
    a_spec = pl.BlockSpec((tm, tk), lambda i, j, k: (i, k))
    b_spec = pl.BlockSpec((tk, tn), lambda i, j, k: (k, j))
    o_spec = pl.BlockSpec((tm, tn), lambda i, j, k: (i, j))
    acc = pltpu.VMEM((tm, tn), F32)
    if mode == "plain":
        kern, in_specs, args, scratch = _mm_plain_kernel, [a_spec, b_spec], (a, b), [acc]
    elif mode == "bias":
        kern = _mm_bias_kernel
        in_specs = [a_spec, b_spec, pl.BlockSpec((1, tn), lambda i, j, k: (0, j))]
        args, scratch = (a, b, bias), [acc]
    elif mode == "swiglu":
        nj = N // tn
        kern = _mm_swiglu_kernel
        in_specs = [a_spec, b_spec, pl.BlockSpec((tk, tn), lambda i, j, k: (k, j + nj))]
        args, scratch = (a, b, b), [acc, acc]
    else:
        kern = functools.partial(_mm_resid_kernel, coef=coef, n_lat=n_lat, tm=tm)
        in_specs = [a_spec, b_spec, o_spec, pl.BlockSpec((2, tn), lambda i, j, k: (0, j))]
        args, scratch = (a, b, res, gate), [acc]
    return pl.pallas_call(
        kern,
        grid=grid,
        in_specs=in_specs,
        out_specs=o_spec,
        out_shape=jax.ShapeDtypeStruct((M, N), out_dtype),
        scratch_shapes=scratch,
        compiler_params=_cparams(("parallel", "parallel", "arbitrary")),
        name=f"mm_{mode}",
    )(*args)


def _rowwise(fn, toks, consts, outs, tm, *, halo=False, name="rowwise"):
    M = toks[0].shape[0]
    n_t, n_c = len(toks), len(consts)

    def kern(*refs):
        i = pl.program_id(0)
        vals = [r[...] for r in refs[:n_t + (2 if halo else 0) + n_c]]
        res = fn(i, *vals)
        for r, v in zip(refs[n_t + (2 if halo else 0) + n_c:], res):
            r[...] = v.astype(r.dtype)

    in_specs = [pl.BlockSpec((tm, t.shape[1]), lambda i: (i, 0)) for t in toks]
    args = list(toks)
    if halo:
        width = toks[0].shape[1]
        per, last = tm // SUBLANES, M // SUBLANES - 1
        in_specs.append(pl.BlockSpec((SUBLANES, width), lambda i: (jnp.maximum(i * per - 1, 0), 0)))
        in_specs.append(pl.BlockSpec((SUBLANES, width), lambda i: (jnp.minimum((i + 1) * per, last), 0)))
        args += [toks[0], toks[0]]
    for c in consts:
        in_specs.append(pl.BlockSpec(c.shape, lambda i, nd=c.ndim: (0,) * nd))
        args.append(c)
    return pl.pallas_call(
        kern,
        grid=(M // tm,),
        in_specs=in_specs,
        out_specs=[pl.BlockSpec((tm, w), lambda i: (i, 0)) for w, _ in outs],
        out_shape=[jax.ShapeDtypeStruct((M, w), dt) for w, dt in outs],
        compiler_params=_cparams(("parallel",)),
        name=name,
    )(*args)


def _row_ids(i, tm, shape):
    return i * tm + lax.broadcasted_iota(jnp.int32, shape, 0)


def _conv3(i, x, prev8, next8, w, tm, n_lat, n_all):
    rows = _row_ids(i, tm, x.shape)
    local = lax.broadcasted_iota(jnp.int32, x.shape, 0)
    prev_row = jnp.broadcast_to(prev8[SUBLANES - 1:SUBLANES, :], x.shape)
    next_row = jnp.broadcast_to(next8[0:1, :], x.shape)
    xp = jnp.where(local == 0, prev_row, pltpu.roll(x, 1, 0))
    xn = jnp.where(local == tm - 1, next_row, pltpu.roll(x, tm - 1, 0))
    xp = jnp.where((rows == 0) | (rows == n_lat), 0.0, xp)
    xn = jnp.where((rows == n_lat - 1) | (rows == n_all - 1), 0.0, xn)
    return xp * w[0:1, :] + x * w[1:2, :] + xn * w[2:3, :]


def _rms(x, gain, width):
    ms = jnp.sum(x * x, axis=-1, keepdims=True) * (1.0 / width)
    return x * lax.rsqrt(ms + NORM_EPS) * gain


def _wkv_kernel(rf, vf, kkf, wf, kf, bf, rb, vb, kkb, wb, kb, bb, eh_ref, dm_ref,
                yf_ref, yb_ref, s_ref, *, T):
    @pl.when(pl.program_id(0) == 0)
    def _():
        s_ref[...] = jnp.zeros_like(s_ref)

    eh = eh_ref[...]
    dm = dm_ref[...]
    tile = (A_HEAD_DIM, WKV_W)
    dirs = ((rf, vf, kkf, wf, kf, bf, yf_ref), (rb, vb, kkb, wb, kb, bb, yb_ref))
    lane_in_head = lax.broadcasted_iota(jnp.int32, tile, 1) % A_HEAD_DIM
    nq, hpq = N_QUADS, WKV_W // A_HEAD_DIM

    def bc(tile8, j, q):
        return jnp.broadcast_to(tile8[j:j + 1, q * WKV_W:(q + 1) * WKV_W], tile)

    def group(g, carry):
        bases = (pl.multiple_of(g * SUBLANES, SUBLANES), pl.multiple_of(T - (g + 1) * SUBLANES, SUBLANES))
        blk = []
        for d, (r_, v_, kk_, w_, k_, b_, _) in enumerate(dirs):
            rows8 = pl.ds(bases[d], SUBLANES)
            blk.append(dict(r=r_[rows8, :], v=v_[rows8, :].astype(BF16), kk=kk_[rows8, :],
                            w=w_[rows8, :], k=k_[rows8, :], b=b_[rows8, :]))
        vk = []
        for d in range(2):
            lhs = [bc(blk[d]["v"], jj, q) * dm for jj in range(SUBLANES) for q in range(nq)]
            vc = _dot(jnp.concatenate(lhs, axis=0), eh)
            vk.append([[vc[(jj * nq + q) * A_HEAD_DIM:(jj * nq + q + 1) * A_HEAD_DIM] * bc(blk[d]["k"], jj, q)
                        for q in range(nq)] for jj in range(SUBLANES)])
        def emit_y(d, row, ysum):
            z = ysum[:A_HEAD_DIM]
            for q in range(1, nq):
                z = jnp.where(lane_in_head == q, ysum[q * A_HEAD_DIM:(q + 1) * A_HEAD_DIM], z)
            zt = jnp.concatenate([z, jnp.zeros_like(z)], axis=0).T
            for h in range(hpq):
                dirs[d][6][bases[d] + row, h] = zt[h * A_HEAD_DIM:h * A_HEAD_DIM + SUBLANES, :]

        for j in range(SUBLANES):
            js = (j, SUBLANES - 1 - j)
            sas = []
            for d in range(2):
                lhs = [(s_ref[d, q] * bc(blk[d]["kk"], js[d], q)).astype(BF16) for q in range(nq)]
                sas.append(_dot(jnp.concatenate(lhs, axis=0), eh))
            for d in range(2):
                qs = []
                for q in range(nq):
                    sa = sas[d][q * A_HEAD_DIM:(q + 1) * A_HEAD_DIM]
                    s = (s_ref[d, q] * bc(blk[d]["w"], js[d], q) - sa * bc(blk[d]["b"], js[d], q)
                         + vk[d][js[d]][q])
                    s_ref[d, q] = s
                    qs.append((s * bc(blk[d]["r"], js[d], q)).astype(BF16))
                emit_y(d, js[d], _dot(jnp.concatenate(qs, axis=0), eh))
        return carry

    lax.fori_loop(0, T // SUBLANES, group, 0)


def _wkv(r, v, kk, w0, k0, b0, w1, k1, b1, eh, dm, n_lat):
    M = r.shape[0]
    T = WKV_T
    nx, nb = n_lat // T, M // T
    nc = nb - nx

    def fwd(i):
        return (jnp.where(i < nc, nx + i, i - nc), 0)

    def bwd(i):
        return (jnp.where(i < nc, nx + (nc - 1 - i), nx - 1 - (i - nc)), 0)

    fspec = pl.BlockSpec((T, A_WIDTH), fwd)
    bspec = pl.BlockSpec((T, A_WIDTH), bwd)
    hpq = WKV_W // A_HEAD_DIM
    yshape = (T, hpq, SUBLANES, LANES)
    ys = pl.pallas_call(
        functools.partial(_wkv_kernel, T=T),
        grid=(nb,),
        in_specs=[fspec] * 6 + [bspec] * 6 + [
            pl.BlockSpec(c.shape, lambda i: (0, 0)) for c in (eh, dm)],
        out_specs=[pl.BlockSpec(yshape, lambda i: fwd(i) + (0, 0)),
                   pl.BlockSpec(yshape, lambda i: bwd(i) + (0, 0))],
        out_shape=[jax.ShapeDtypeStruct((M,) + yshape[1:], F32)] * 2,
        scratch_shapes=[pltpu.VMEM((2, N_QUADS, A_HEAD_DIM, WKV_W), F32)],
        compiler_params=_cparams(("arbitrary",)),
        name="wkv7_scan",
    )(r, v, kk, w0, k0, b0, r, v, kk, w1, k1, b1, eh, dm)
    return [y[:, :, :N_QUADS, :A_HEAD_DIM].transpose(0, 2, 1, 3).reshape(M, A_WIDTH) for y in ys]


def _dft_rows_kernel(fc_ref, fs_ref, x_ref, ar_ref, ai_ref):
    x = x_ref[...].astype(BF16)
    ar_ref[...] = _dot(fc_ref[...], x)
    ai_ref[...] = -_dot(fs_ref[...], x)


def _dft_rows(fc, fs, x2d):
    W = x2d.shape[1]
    n1, n1h = fc.shape
    tn = _pick(W, (4096, 2048, 1024, 512, 256, 128))
    fspec = pl.BlockSpec(fc.shape, lambda j: (0, 0))
    return pl.pallas_call(
        _dft_rows_kernel,
        grid=(W // tn,),
        in_specs=[fspec, fspec, pl.BlockSpec((n1h, tn), lambda j: (0, j))],
        out_specs=[pl.BlockSpec((n1, tn), lambda j: (0, j))] * 2,
        out_shape=[jax.ShapeDtypeStruct((n1, W), F32)] * 2,
        compiler_params=_cparams(("parallel",)),
        name="fft_stage1",
    )(fc, fs, x2d)


def _tile_lanes(t, width):
    return t if width == LANES else jnp.concatenate([t] * (width // LANES), axis=-1)


def _cdot(m, xr, xi):
    out = _dot(m, jnp.concatenate([xr, xi], axis=0).astype(BF16))
    return out[:FFT_N2], out[FFT_N2:]


def _fft_fwd_mid(ar, ai, twr, twi, mf):
    return _cdot(mf, ar * twr - ai * twi, ar * twi + ai * twr)


def _filter_spec_kernel(afr, afi, abr, abi, twr_ref, twi_ref, shr_ref, shi_ref, mf_ref, hr_ref, hi_ref):
    ct = afr.shape[-1]
    twr, twi = _tile_lanes(twr_ref[0], ct), _tile_lanes(twi_ref[0], ct)
    shr, shi = _tile_lanes(shr_ref[0], ct), _tile_lanes(shi_ref[0], ct)
    mf = mf_ref[...]
    fr, fi = _fft_fwd_mid(afr[0], afi[0], twr, twi, mf)
    br, bi = _fft_fwd_mid(abr[0], abi[0], twr, twi, mf)
    hr_ref[0] = fr + br * shr + bi * shi
    hi_ref[0] = fi + br * shi - bi * shr


def _conv_spec_kernel(agr, agi, hr, hi, twr_ref, twi_ref, mf_ref, mi_ref, dr_ref, di_ref):
    ct = agr.shape[-1]
    twr, twi = _tile_lanes(twr_ref[0], ct), _tile_lanes(twi_ref[0], ct)
    xr, xi = _fft_fwd_mid(agr[0], agi[0], twr, twi, mf_ref[...])
    yr = xr * hr[0] - xi * hi[0]
    yi = xr * hi[0] + xi * hr[0]
    cr, ci = _cdot(mi_ref[...], yr, yi)
    dr_ref[0] = cr * twr + ci * twi
    di_ref[0] = ci * twr - cr * twi


def _idft_rows_kernel(gc_ref, gs_ref, dr_ref, di_ref, y_ref, *, scale):
    y_ref[...] = scale * (_dot(gc_ref[...], dr_ref[...].astype(BF16))
                          - _dot(gs_ref[...], di_ref[...].astype(BF16)))


def _idft_rows(gc, gs, dr2d, di2d, scale):
    n1, W = dr2d.shape
    n1h = gc.shape[0]
    tn = _pick(W, (4096, 2048, 1024, 512, 256, 128))
    gspec = pl.BlockSpec(gc.shape, lambda j: (0, 0))
    dspec = pl.BlockSpec((n1, tn), lambda j: (0, j))
    return pl.pallas_call(
        functools.partial(_idft_rows_kernel, scale=scale),
        grid=(W // tn,),
        in_specs=[gspec, gspec, dspec, dspec],
        out_specs=pl.BlockSpec((n1h, tn), lambda j: (0, j)),
        out_shape=jax.ShapeDtypeStruct((n1h, W), F32),
        compiler_params=_cparams(("parallel",)),
        name="fft_stage1_inv",
    )(gc, gs, dr2d, di2d)


def _fft_tables(n_lat):
    n = 2 * n_lat
    n1 = n // FFT_N2

    def cs(rows, cols, period):
        idx = (jnp.arange(rows, dtype=jnp.int32)[:, None] * jnp.arange(cols, dtype=jnp.int32)[None, :]) % period
        ang = idx.astype(F32) * (2.0 * math.pi / period)
        return jnp.cos(ang), jnp.sin(ang)

    n1h = n1 // 2
    nk = n1h + 1
    nkp = -(-nk // SUBLANES) * SUBLANES
    live = (jnp.arange(nkp) < nk).astype(F32)[:, None]
    f1c, f1s = cs(nkp, n1h, n1)
    f1c, f1s = f1c * live, f1s * live
    f2c, f2s = cs(FFT_N2, FFT_N2, FFT_N2)
    twc, tws = cs(nkp, FFT_N2, n)
    shape = (nkp, FFT_N2, LANES)
    twc = jnp.broadcast_to(twc[:, :, None], shape)
    tws = jnp.broadcast_to(tws[:, :, None], shape)
    kidx = (jnp.arange(nkp, dtype=jnp.int32)[:, None] + n1 * jnp.arange(FFT_N2, dtype=jnp.int32)[None, :]) % n
    kang = kidx.astype(F32) * (2.0 * math.pi / n)
    shr = jnp.broadcast_to(jnp.cos(kang)[:, :, None], shape)
    shi = jnp.broadcast_to(jnp.sin(kang)[:, :, None], shape)
    gc, gs = cs(n1h, nkp, n1)
    k1 = jnp.arange(nkp)
    ck = jnp.where((k1 == 0) | (k1 == n1h), 1.0, 2.0) * (k1 < nk)
    gc, gs = gc * ck[None, :], gs * ck[None, :]

    mf = jnp.block([[f2c, f2s], [-f2s, f2c]]).astype(BF16)
    mi = jnp.block([[f2c, -f2s], [f2s, f2c]]).astype(BF16)
    return dict(n=n, n1=n1, nkp=nkp, f1c=f1c.astype(BF16), f1s=f1s.astype(BF16),
                gc=gc.astype(BF16), gs=gs.astype(BF16), mf=mf, mi=mi,
                twr=twc, twi=-tws, shr=shr, shi=shi)


def _long_conv(g, h, n_lat, tabs):
    M, C = g.shape
    n, nkp = tabs["n"], tabs["nkp"]
    aqr, aqi = _dft_rows(tabs["f1c"], tabs["f1s"], h.reshape(n_lat // FFT_N2, FFT_N2 * 2 * C))
    aqr = aqr.reshape(nkp, FFT_N2, 2 * C)
    aqi = aqi.reshape(nkp, FFT_N2, 2 * C)
    ct = _pick(C, (1280, 512, 256, 128))
    nj = C // ct
    blk_f = pl.BlockSpec((1, FFT_N2, ct), lambda k, j: (k, 0, j))
    blk_p = pl.BlockSpec((1, FFT_N2, ct), lambda k, j: (k, 0, j + nj))
    tw = pl.BlockSpec((1, FFT_N2, LANES), lambda k, j: (k, 0, 0))
    fm = pl.BlockSpec(tabs["mf"].shape, lambda k, j: (0, 0))
    hr, hi = pl.pallas_call(
        _filter_spec_kernel,
        grid=(nkp, nj),
        in_specs=[blk_f, blk_f, blk_p, blk_p, tw, tw, tw, tw, fm],
        out_specs=[blk_f, blk_f],
        out_shape=[jax.ShapeDtypeStruct((nkp, FFT_N2, C), F32)] * 2,
        compiler_params=_cparams(("parallel", "parallel")),
        name="fft_filter_spectrum",
    )(aqr, aqi, aqr, aqi, tabs["twr"], tabs["twi"], tabs["shr"], tabs["shi"], tabs["mf"])
    agr, agi = _dft_rows(tabs["f1c"], tabs["f1s"], g.reshape(M // FFT_N2, FFT_N2 * C))
    agr = agr.reshape(nkp, FFT_N2, C)
    agi = agi.reshape(nkp, FFT_N2, C)
    dr, di = pl.pallas_call(
        _conv_spec_kernel,
        grid=(nkp, nj),
        in_specs=[blk_f] * 4 + [tw, tw, fm, fm],
        out_specs=[blk_f, blk_f],
        out_shape=[jax.ShapeDtypeStruct((nkp, FFT_N2, C), F32)] * 2,
        compiler_params=_cparams(("parallel", "parallel")),
        name="fft_conv_spectrum",
    )(agr, agi, hr, hi, tabs["twr"], tabs["twi"], tabs["mf"], tabs["mi"])
    y2d = _idft_rows(tabs["gc"], tabs["gs"], dr.reshape(nkp, FFT_N2 * C), di.reshape(nkp, FFT_N2 * C),
                     1.0 / n)
    return y2d.reshape(n_lat, C)


def _ctx_conv_kernel(g_ref, hf_ref, hb_ref, y_ref, *, n):
    g = g_ref[...]
    rows = lax.broadcasted_iota(jnp.int32, g.shape, 0)
    y_ref[...] = jnp.zeros_like(y_ref)

    def causal(d, carry):
        y_ref[...] += jnp.where(rows >= d, pltpu.roll(g, d, 0), 0.0) * hf_ref[pl.ds(d, 1), :]
        return carry

    def anticausal(d, carry):
        y_ref[...] += jnp.where(rows < n - d, pltpu.roll(g, n - d, 0), 0.0) * hb_ref[pl.ds(d - 1, 1), :]
        return carry

    lax.fori_loop(0, n, causal, 0)
    lax.fori_loop(1, n, anticausal, 0)


def _ctx_conv(g, h, n_lat, n_ctx):
    C = g.shape[1]
    ct = _pick(C, (256, 128))
    nj = C // ct
    rb = n_lat // n_ctx
    return pl.pallas_call(
        functools.partial(_ctx_conv_kernel, n=n_ctx),
        grid=(nj,),
        in_specs=[pl.BlockSpec((n_ctx, ct), lambda j: (rb, j)),
                  pl.BlockSpec((n_ctx, ct), lambda j: (0, j)),
                  pl.BlockSpec((n_ctx, ct), lambda j: (0, j + nj))],
        out_specs=pl.BlockSpec((n_ctx, ct), lambda j: (0, j)),
        out_shape=jax.ShapeDtypeStruct((n_ctx, C), F32),
        compiler_params=_cparams(("parallel",)),
        name="hyena_ctx_conv",
    )(g, h, h)


def _filter_feats(length):
    t = jnp.linspace(0.0, 1.0, length, dtype=F32)[:, None]
    w = (2.0 * math.pi / length) * jnp.arange(length, dtype=F32)[:, None]
    f = jnp.linspace(1e-4, FILTER_BANDS - 1, FILTER_BANDS, dtype=F32)[None, :]
    z = jnp.concatenate([t, jnp.cos(f * w), -jnp.sin(f * w)], axis=-1)
    return jnp.pad(z, ((0, 0), (0, LANES - FILTER_EMB)))


def _filter_kernel(z_ref, w1, b1, w2, b2, w3, b3, w4, fr, dl, h_ref, s_ref, *, tm, length):
    i = pl.program_id(0)
    z = z_ref[...]
    freq = fr[...]
    h = jnp.sin(freq * (_dot3(z, w1[...]) + b1[...]))
    h = jnp.sin(freq * (_dot3(h, w2[...]) + b2[...]))
    h = jnp.sin(freq * (_dot3(h, w3[...]) + b3[...]))
    h = _dot3(h, w4[...])
    h = h * jnp.exp(-z[:, 0:1] * dl[...])
    rows = _row_ids(i, tm, h.shape)
    cols = lax.broadcasted_iota(jnp.int32, h.shape, 1)
    keep = (cols < B_WIDTH) | (rows < length - 1)
    h = jnp.where(keep, h, 0.0)
    h_ref[...] = h
    part = jnp.sum(jnp.abs(h), axis=0, keepdims=True)

    @pl.when(i == 0)
    def _():
        s_ref[...] = jnp.zeros_like(s_ref)

    s_ref[...] += jnp.broadcast_to(part, s_ref.shape)


def _filter_taps(length, fp):
    z = _filter_feats(length)
    tm = _pick(length, (256, 128))
    consts = [fp["w1"], fp["b1"], fp["w2"], fp["b2"], fp["w3"], fp["b3"], fp["w4"], fp["freq"], fp["deltas"]]
    in_specs = [pl.BlockSpec((tm, LANES), lambda i: (i, 0))]
    in_specs += [pl.BlockSpec(c.shape, lambda i: (0, 0)) for c in consts]
    h, s = pl.pallas_call(
        functools.partial(_filter_kernel, tm=tm, length=length),
        grid=(length // tm,),
        in_specs=in_specs,
        out_specs=[pl.BlockSpec((tm, 2 * B_WIDTH), lambda i: (i, 0)),
                   pl.BlockSpec((SUBLANES, 2 * B_WIDTH), lambda i: (0, 0))],
        out_shape=[jax.ShapeDtypeStruct((length, 2 * B_WIDTH), F32),
                   jax.ShapeDtypeStruct((SUBLANES, 2 * B_WIDTH), F32)],
        compiler_params=_cparams(("arbitrary",)),
        name="hyena_filter",
    )(z, *consts)
    return h, s


def _attn_kernel(q_ref, k_ref, v_ref, o_ref, m_ref, acc_ref, *, tk):
    m_ref[...] = jnp.full_like(m_ref, -jnp.inf)
    acc_ref[...] = jnp.zeros_like(acc_ref)
    for c0 in range(0, k_ref.shape[0], tk):
        k, v = k_ref[c0:c0 + tk, :], v_ref[c0:c0 + tk, :]
        for r0 in range(0, q_ref.shape[0], ATTN_SUB):
            rs = slice(r0, r0 + ATTN_SUB)
            s = lax.dot_general(q_ref[rs, :], k, (((1,), (1,)), ((), ())), preferred_element_type=F32)
            m_old = m_ref[rs, :]
            m_new = jnp.maximum(m_old, jnp.max(s, axis=-1, keepdims=True))
            alpha = jnp.exp(m_old - m_new)
            p = jnp.exp(s - m_new)
            acc_ref[rs, :] = alpha * acc_ref[rs, :] + _dot(p.astype(BF16), v)
            m_ref[rs, :] = m_new
    acc = acc_ref[...]
    o_ref[...] = (acc[:, :V_HEAD] / acc[:, V_HEAD:V_HEAD + 1]).astype(o_ref.dtype)


def _attention(q, k, v, q_row0, n_q, k_row0, n_k):
    tq = _pick(n_q, (1024, 512, 256))
    tk = _pick(n_k, (768, 1024, 640, 512, 256))
    assert q_row0 % tq == 0 and k_row0 % n_k == 0
    qb, kb = q_row0 // tq, k_row0 // n_k
    return pl.pallas_call(
        functools.partial(_attn_kernel, tk=tk),
        grid=(C_HEADS, n_q // tq),
        in_specs=[pl.BlockSpec((tq, QH), lambda h, i: (qb + i, h)),
                  pl.BlockSpec((n_k, QH), lambda h, i: (kb, h)),
                  pl.BlockSpec((n_k, QH), lambda h, i: (kb, h))],
        out_specs=pl.BlockSpec((tq, V_HEAD), lambda h, i: (i, h)),
        out_shape=jax.ShapeDtypeStruct((n_q, C_WIDTH), BF16),
        scratch_shapes=[pltpu.VMEM((tq, 1), F32), pltpu.VMEM((tq, QH), F32)],
        compiler_params=_cparams(("parallel", "parallel")),
        name="mla_attention",
    )(q, k, v)


def _pair_sum(x, eh):
    return jnp.concatenate(
        [_dot2(x[:, p * LANES:(p + 1) * LANES], eh) for p in range(N_PAIRS)], axis=-1)


def _rot_half(t):
    lane = lax.broadcasted_iota(jnp.int32, t.shape, 1)
    up = pltpu.roll(t, LANES - 16, 1)
    down = pltpu.roll(t, 16, 1)
    return jnp.where(lane % 32 < 16, -up, down)


def _layer(xa, mod, p, const, n_lat, need_ctx):
    M = xa.shape[0]
    eh = const["eh"]

    def modrow(k):
        return mod[:, k * D_MODEL:(k + 1) * D_MODEL]

    def norm_mod(x, gain, shift, scale):
        tm = 256

        def fn(i, xb, g, sh, sc):
            rows = _row_ids(i, tm, xb.shape)
            lat = rows < n_lat
            y = _rms(xb, g, D_MODEL)
            return (y * (1.0 + jnp.where(lat, sc[0:1], sc[1:2])) + jnp.where(lat, sh[0:1], sh[1:2]),)

        return _rowwise(fn, [x], [gain, shift, scale], [(D_MODEL, BF16)], tm, name="norm_modulate")[0]

    def ffn(x, gain, w_gu, w_down, k0):
        h = norm_mod(x, gain, modrow(k0), modrow(k0 + 1))
        act = _mm(h, w_gu, mode="swiglu", out_dtype=BF16, tn=1024)
        return _mm(act, w_down, mode="resid", res=x, gate=modrow(k0 + 2), coef=0.5, n_lat=n_lat)

    xa = ffn(xa, p["norm_ffn1"], p["ffn1_gu"], p["ffn1_down"], 0)

    h = norm_mod(xa, p["norm_mix"], modrow(3), modrow(4))
    pa = _mm(h, p["w_in_a"], tn=1408)
    pb = _mm(h, p["w_in_b"], tn=1280)
    pc = _mm(h, p["w_in_c"], tn=C_PAD)

    tma = 128

    def rwkv_prep(i, x, prev8, next8, cw, k_k, k_a, w0, wup0, wup1, a0, aup0, aup1, gup, eh_):
        u = _conv3(i, x, prev8, next8, cw, tma, n_lat, M)
        r, k, v = u[:, :A_WIDTH], u[:, A_WIDTH:2 * A_WIDTH], u[:, 2 * A_WIDTH:3 * A_WIDTH]
        xwa = u[:, 3 * A_WIDTH:3 * A_WIDTH + LANES]
        xg = u[:, 3 * A_WIDTH + LANES:3 * A_WIDTH + 3 * LANES]
        kk = k * k_k
        kk = kk / jnp.maximum(jnp.sqrt(_pair_sum(kk * kk, eh_)), 1e-12)
        g = _dot3(_sigmoid(xg), gup)
        th = jnp.tanh(xwa)
        outs = [r, v, kk, g]
        for wup, aup, d in ((wup0, aup0, 0), (wup1, aup1, 1)):
            logw = -_softplus(-(w0[d:d + 1] + _dot3(th, wup))) - 0.5
            decay = jnp.exp(-jnp.exp(logw))
            a = _sigmoid(a0[d:d + 1] + _dot3(xwa, aup))
            outs += [decay, k * (1.0 + (a - 1.0) * k_a), kk * a]
        return outs

    r, v, kk, g, w0, k0, b0, w1, k1, b1 = _rowwise(
        rwkv_prep, [pa],
        [p["a_conv"], p["a_k_k"], p["a_k_a"], p["a_w0"], p["a_w_up0"], p["a_w_up1"], p["a_a0"],
         p["a_a_up0"], p["a_a_up1"], p["a_g_up"], eh],
        [(A_WIDTH, F32)] * 10, tma, halo=True, name="rwkv_prep")
    yf, yb = _wkv(r, v, kk, w0, k0, b0, w1, k1, b1, const["eh2"], const["dm2"], n_lat)

    def rwkv_out(i, yf_, yb_, r_, v_, g_, k0_, k1_, r_k, ln_w, ln_b, eh_):
        y = yf_ + yb_
        inv = 1.0 / A_HEAD_DIM
        mu = _pair_sum(y, eh_) * inv
        yc = y - mu
        var = _pair_sum(yc * yc, eh_) * inv
        yn = yc * lax.rsqrt(var + GN_EPS) * ln_w + ln_b
        bonus = _pair_sum(r_ * (0.5 * (k0_ + k1_)) * r_k, eh_) * v_
        return ((yn + bonus) * g_,)

    o_a = _rowwise(rwkv_out, [yf, yb, r, v, g, k0, k1], [p["a_r_k"], p["a_ln_w"], p["a_ln_b"], eh],
                   [(A_WIDTH, BF16)], 256, name="rwkv_out")[0]

    tmb = 256

    def hyena_prep(i, x, prev8, next8, cw, cb):
        u = _conv3(i, x, prev8, next8, cw, tmb, n_lat, M) + cb
        return u[:, :B_WIDTH], u[:, B_WIDTH:2 * B_WIDTH] * u[:, 2 * B_WIDTH:]

    x0, gsig = _rowwise(hyena_prep, [pb], [p["b_conv"], p["b_conv_b"]],
                        [(B_WIDTH, F32)] * 2, tmb, halo=True, name="hyena_prep")
    n_ctx = M - n_lat
    hx, sx = _filter_taps(n_lat, p["filt"])
    hc, sc = _filter_taps(n_ctx, p["filt"])
    y_lat = _long_conv(gsig, hx, n_lat, const["fft"])
    y_ctx = _ctx_conv(gsig, hc, n_lat, n_ctx) if need_ctx else jnp.zeros((n_ctx, B_WIDTH), F32)
    yconv = jnp.concatenate([y_lat, y_ctx], axis=0)

    def hyena_out(i, x0_, g_, y_, bias, sx_, sc_):
        rows = _row_ids(i, tmb, x0_.shape)
        nx_ = sx_[0:1, :B_WIDTH] + sx_[0:1, B_WIDTH:]
        nc_ = sc_[0:1, :B_WIDTH] + sc_[0:1, B_WIDTH:]
        norm = jnp.where(rows < n_lat, nx_, nc_)
        return (x0_ * (y_ / norm + g_ * bias),)

    o_b = _rowwise(hyena_out, [x0, gsig, yconv], [p["b_bias"], sx, sc], [(B_WIDTH, BF16)], tmb,
                   name="hyena_out")[0]

    tmc = 256

    def mla_prep(i, x, cos, sin, gq, gkv, gpe):
        cq, ckv = x[:, :Q_RANK], x[:, Q_RANK:Q_RANK + KV_RANK]
        kpe = x[:, Q_RANK + KV_RANK:]
        kpe = _rms(kpe, gpe, QK_ROPE)
        kpe = kpe * cos + _rot_half(kpe) * sin
        return _rms(cq, gq, Q_RANK), _rms(ckv, gkv, KV_RANK), kpe

    cqn, ckvn, kpe = _rowwise(mla_prep, [pc, const["cos"], const["sin"]],
                              [p["c_q_norm"], p["c_kv_norm"], p["c_kn_pe"]],
                              [(Q_RANK, BF16), (KV_RANK, BF16), (LANES, F32)], tmc, name="mla_prep")
    q_raw = _mm(cqn, p["c_q_up"], tn=768)
    kv_raw = _mm(ckvn, p["c_kv_up"], tn=768)
    scale = 1.0 / math.sqrt(QK_DIM)

    def mla_heads(i, qr, kvr, kpe_, cos, sin, gn_q, gpe_q, gn_k):
        qs, ks, vs = [], [], []
        for hh in range(C_HEADS):
            qn = _rms(qr[:, hh * QH:hh * QH + QK_NOPE], gn_q, QK_NOPE)
            qp = _rms(qr[:, hh * QH + QK_NOPE:(hh + 1) * QH], gpe_q, QK_ROPE)
            qp = qp * cos + _rot_half(qp) * sin
            qs += [qn * scale, qp * scale]
            ks += [_rms(kvr[:, hh * QH:hh * QH + QK_NOPE], gn_k, QK_NOPE), kpe_]
            vh = kvr[:, hh * QH + QK_NOPE:(hh + 1) * QH]
            vs += [vh, jnp.ones_like(vh)]
        return jnp.concatenate(qs, axis=-1), jnp.concatenate(ks, axis=-1), jnp.concatenate(vs, axis=-1)

    q, k, vv = _rowwise(mla_heads, [q_raw, kv_raw, kpe, const["cos"], const["sin"]],
                        [p["c_qn_nope"], p["c_qn_pe"], p["c_kn_nope"]],
                        [(C_HEADS * QH, BF16)] * 3, tmc, name="mla_heads")
    oc_x = _attention(q, k, vv, 0, n_lat, 0, M)
    if need_ctx:
        oc_c = _attention(q, k, vv, n_lat, n_ctx, n_lat, n_ctx)
    else:
        oc_c = jnp.zeros((n_ctx, C_WIDTH), BF16)
    o_c = jnp.concatenate([oc_x, oc_c], axis=0)

    o = jnp.concatenate([o_a, o_b, o_c], axis=-1)
    xa = _mm(o, p["w_out"], mode="resid", res=xa, gate=modrow(5), coef=1.0, n_lat=n_lat)
    return ffn(xa, p["norm_ffn2"], p["ffn2_gu"], p["ffn2_down"], 6)


def _pad_cols(w, width):
    return jnp.pad(w, [(0, 0)] * (w.ndim - 1) + [(0, width - w.shape[-1])])


def _pad_rows(w, height):
    return jnp.pad(w, [(0, 0)] * (w.ndim - 2) + [(0, height - w.shape[-2]), (0, 0)])


def _row(v):
    return v.reshape(1, -1)


def kernel(x, c, ctx, c_ctx, ada_down, ada_up, ada_bias, norm_ffn1, norm_mix, norm_ffn2, ffn1_gu, ffn1_down, ffn2_gu, ffn2_down, w_in, w_out, a_conv, a_w0, a_w_up, a_a0, a_a_up, a_g_up, a_k_k, a_k_a, a_r_k, a_ln_w, a_ln_b, b_conv, b_conv_b, b_fw1, b_fb1, b_fw2, b_fb2, b_fw3, b_fb3, b_fw4, b_freq, b_bias, c_q_norm, c_q_up, c_kv_norm, c_kv_up, c_qn_nope, c_qn_pe, c_kn_nope, c_kn_pe):
    n_lat, n_ctx = x.shape[1], ctx.shape[1]
    depth = w_in.shape[0]
    xa = jnp.concatenate([x[0], ctx[0]], axis=0)

    lane = jnp.arange(LANES)
    eh = (lane[:, None] // A_HEAD_DIM == lane[None, :] // A_HEAD_DIM).astype(BF16)
    lane2 = jnp.arange(WKV_W)
    eh2 = (lane2[:, None] // A_HEAD_DIM == lane2[None, :] // A_HEAD_DIM).astype(BF16)
    dm2 = (lane2[None, :] % A_HEAD_DIM == jnp.arange(A_HEAD_DIM)[:, None]).astype(BF16)
    rows_g = n_lat // GRID_W
    row = jnp.broadcast_to(jnp.arange(rows_g, dtype=F32)[:, None], (rows_g, GRID_W)).reshape(-1)
    col = jnp.broadcast_to(jnp.arange(GRID_W, dtype=F32)[None, :], (rows_g, GRID_W)).reshape(-1)
    half = QK_ROPE // 2
    inv = ROPE_THETA ** (-jnp.arange(0, half, 2, dtype=F32) / half)
    ang = jnp.concatenate([row[:, None] * inv, row[:, None] * inv, col[:, None] * inv, col[:, None] * inv], axis=-1)
    cos = jnp.concatenate([jnp.cos(ang), jnp.ones((n_ctx, QK_ROPE), F32)], axis=0)
    sin = jnp.concatenate([jnp.sin(ang), jnp.zeros((n_ctx, QK_ROPE), F32)], axis=0)
    const = dict(eh=eh, eh2=eh2, dm2=dm2, cos=_pad_cols(cos, LANES), sin=_pad_cols(sin, LANES), fft=_fft_tables(n_lat))
    deltas = jnp.abs(jnp.linspace(MIN_DECAY, MAX_DECAY, B_WIDTH, dtype=F32))

    cc = jnp.pad(jnp.concatenate([c, c_ctx[None, :]], axis=0), ((0, 14), (0, 0)))
    cs = _rowwise(lambda i, t: (_silu(t),), [cc], [], [(D_MODEL, BF16)], 16, name="silu")[0]

    q_up = c_q_up.reshape(depth, Q_RANK, C_HEADS, QK_DIM)
    q_up = jnp.pad(q_up, ((0, 0), (0, 0), (0, 0), (0, QH - QK_DIM))).reshape(depth, Q_RANK, C_HEADS * QH)

    for l in range(depth):
        mid = _mm(cs, ada_down[l].astype(BF16), out_dtype=BF16, tn=256)
        mod = _mm(mid, ada_up[l].astype(BF16), mode="bias", bias=_row(ada_bias[l]), tn=1024, tk=256)[:2]
        w_in_l = w_in[l]
        filt = dict(
            w1=_pad_cols(_pad_rows(b_fw1[l], LANES), LANES), b1=_pad_cols(_row(b_fb1[l]), LANES),
            w2=_pad_cols(_pad_rows(b_fw2[l], LANES), LANES), b2=_pad_cols(_row(b_fb2[l]), LANES),
            w3=_pad_cols(_pad_rows(b_fw3[l], LANES), LANES), b3=_pad_cols(_row(b_fb3[l]), LANES),
            w4=_pad_rows(b_fw4[l], LANES), freq=_pad_cols(_row(b_freq[l]), LANES),
            deltas=_row(jnp.tile(deltas, 2)))
        zpad = jnp.zeros((A_DECAY_RANK, A_WIDTH), F32)
        p = dict(
            norm_ffn1=_row(norm_ffn1[l]), norm_mix=_row(norm_mix[l]), norm_ffn2=_row(norm_ffn2[l]),
            ffn1_gu=ffn1_gu[l].astype(BF16), ffn1_down=ffn1_down[l].astype(BF16),
            ffn2_gu=ffn2_gu[l].astype(BF16), ffn2_down=ffn2_down[l].astype(BF16),
            w_in_a=_pad_cols(w_in_l[:, :A_IN], A_PAD).astype(BF16),
            w_in_b=w_in_l[:, A_IN:A_IN + B_IN].astype(BF16),
            w_in_c=_pad_cols(w_in_l[:, A_IN + B_IN:], C_PAD).astype(BF16),
            w_out=w_out[l].astype(BF16),
            a_conv=_pad_cols(a_conv[l], A_PAD), a_k_k=_row(a_k_k[l]), a_k_a=_row(a_k_a[l]),
            a_w0=a_w0[l], a_a0=a_a0[l],
            a_w_up0=jnp.concatenate([a_w_up[l, 0], zpad], axis=0),
            a_w_up1=jnp.concatenate([a_w_up[l, 1], zpad], axis=0),
            a_a_up0=jnp.concatenate([zpad, a_a_up[l, 0]], axis=0),
            a_a_up1=jnp.concatenate([zpad, a_a_up[l, 1]], axis=0),
            a_g_up=_pad_rows(a_g_up[l], 2 * LANES),
            a_r_k=_row(a_r_k[l]), a_ln_w=_row(a_ln_w[l]), a_ln_b=_row(a_ln_b[l]),
            b_conv=b_conv[l], b_conv_b=_row(b_conv_b[l]), b_bias=_row(b_bias[l]), filt=filt,
            c_q_norm=_row(c_q_norm[l]), c_kv_norm=_row(c_kv_norm[l]),
            c_kn_pe=_pad_cols(_row(c_kn_pe[l]), LANES),
            c_q_up=q_up[l].astype(BF16), c_kv_up=c_kv_up[l].astype(BF16),
            c_qn_nope=_row(c_qn_nope[l]), c_qn_pe=_pad_cols(_row(c_qn_pe[l]), LANES),
            c_kn_nope=_row(c_kn_nope[l]),
        )
        xa = _layer(xa, mod, p, const, n_lat, l < depth - 1)
    return xa[:n_lat][None]
```
